```python
import math
import jax
import jax.numpy as jnp
from jax import lax
import numpy as np

D_MODEL = 1024
BATCH = 2
SEQ = 8192
DEPTH = 4
DEC_BATCH = 128
DEC_SEQ = 8
PAST_LEN = 8192
PAGE_SIZE = 128

H_A = 4
DK_A = 32
DV_A = 64
H_B = 4
DK_B = 64
DV_B = 128
GATE_RANK = 16
GATE_TAU = 16.0
GLA_CHUNK = 64
H_C = 4
Q_LORA = 192
KV_LORA = 128
NOPE_C = 64
ROPE_C = 32
DV_C = 64
ROPE_THETA = 10000.0
D_MIX = H_A * DV_A + H_B * DV_B + H_C * DV_C
D_IN = 2 * H_A * 2 * DK_A + H_A * DV_A + 2 * H_B * DK_B + 2 * H_B * DV_B + GATE_RANK + Q_LORA + KV_LORA + ROPE_C
D_FF = 2816
CONV_W = 3
N_BUCKETS = 32
MAX_DISTANCE = 128
Q_BLOCK = 128
NORM_EPS = 1e-6

kernel_name = 'hybrid_diff_gla_mla_step'


def split_points():
    sizes = [H_A * 2 * DK_A, H_A * 2 * DK_A, H_A * DV_A,
             H_B * DK_B, H_B * DK_B, H_B * DV_B, H_B * DV_B, GATE_RANK,
             Q_LORA, KV_LORA, ROPE_C]
    return [int(s) for s in np.cumsum(sizes)[:-1]]


def rmsnorm(x, w):
    xf = x.astype(jnp.float32)
    y = xf * lax.rsqrt(jnp.mean(xf * xf, axis=-1, keepdims=True) + NORM_EPS)
    return (y * w.astype(jnp.float32)).astype(x.dtype)


def rope(x, pos):
    half = x.shape[-1] // 2
    inv = ROPE_THETA ** (-jnp.arange(half, dtype=jnp.float32) / half)
    ang = pos.astype(jnp.float32)[:, None] * inv
    ang = ang.reshape(ang.shape[:1] + (1,) * (x.ndim - 3) + (half,))
    cos, sin = jnp.cos(ang), jnp.sin(ang)
    xf = x.astype(jnp.float32)
    x1, x2 = xf[..., :half], xf[..., half:]
    return jnp.concatenate([x1 * cos - x2 * sin, x1 * sin + x2 * cos], axis=-1).astype(x.dtype)


def t5_bucket(rel):
    n = jnp.maximum(rel, 0)
    max_exact = N_BUCKETS // 2
    nf = jnp.maximum(n, max_exact).astype(jnp.float32)
    large = max_exact + (jnp.log(nf / max_exact) / math.log(MAX_DISTANCE / max_exact)
                         * (N_BUCKETS - max_exact)).astype(jnp.int32)
    large = jnp.minimum(large, N_BUCKETS - 1)
    return jnp.where(n < max_exact, n, large)


def diff_attention(q, k, v, q_pos, k_pos, rel_table, lam):
    B, T = q.shape[:2]
    qb = math.gcd(T, Q_BLOCK)
    nb = T // qb
    scale = DK_A ** -0.5

    def block(start):
        qs = lax.dynamic_slice_in_dim(q, start, qb, axis=1)
        ps = lax.dynamic_slice_in_dim(q_pos, start, qb, axis=0)
        s = jnp.einsum('bqhmd,bkhmd->bhmqk', qs, k).astype(jnp.float32) * scale
        rel = ps[:, None] - k_pos[None, :]
        bias = jnp.moveaxis(rel_table[t5_bucket(rel)].astype(jnp.float32), -1, 0)
        s = jnp.where(rel >= 0, s + bias[None, :, None], -jnp.inf)
        pr = jax.nn.softmax(s, axis=-1)
        a = pr[:, :, 0] - lam * pr[:, :, 1]
        return jnp.einsum('bhqk,bkhv->bqhv', a.astype(v.dtype), v)

    o = lax.map(block, jnp.arange(nb, dtype=jnp.int32) * qb)
    return jnp.moveaxis(o, 0, 1).reshape(B, T, q.shape[2], v.shape[-1])


def mla_attention(q_lat, q_rope, c_kv, k_rope, q_pos, k_pos):
    B, T, H = q_lat.shape[:3]
    qb = math.gcd(T, Q_BLOCK)
    nb = T // qb
    scale = (NOPE_C + ROPE_C) ** -0.5

    def block(start):
        ql = lax.dynamic_slice_in_dim(q_lat, start, qb, axis=1)
        qr = lax.dynamic_slice_in_dim(q_rope, start, qb, axis=1)
        ps = lax.dynamic_slice_in_dim(q_pos, start, qb, axis=0)
        s = (jnp.einsum('bqhc,bkc->bhqk', ql, c_kv)
             + jnp.einsum('bqhr,bkr->bhqk', qr, k_rope)).astype(jnp.float32) * scale
        s = jnp.where(ps[:, None] >= k_pos[None, :], s, -jnp.inf)
        pr = jax.nn.softmax(s, axis=-1)
        return jnp.einsum('bhqk,bkc->bqhc', pr.astype(c_kv.dtype), c_kv)

    o = lax.map(block, jnp.arange(nb, dtype=jnp.int32) * qb)
    return jnp.moveaxis(o, 0, 1).reshape(B, T, H, c_kv.shape[-1])


def gla_recurrence(q, k, v, log_a, s0):
    B, T, H, DK = q.shape
    DV = v.shape[-1]
    C = math.gcd(T, GLA_CHUNK)
    n = T // C

    def chunks(a):
        return jnp.moveaxis(a.astype(jnp.float32).reshape(B, n, C, H, a.shape[-1]), 1, 0)

    causal = jnp.tril(jnp.ones((C, C), dtype=bool))

    def step(S, inp):
        qc, kc, vc, gc = inp
        b = jnp.cumsum(gc, axis=1)
        rel = jnp.where(causal[None, :, :, None, None], b[:, :, None] - b[:, None, :], -jnp.inf)
        A = jnp.einsum('bihd,bjhd,bijhd->bhij', qc, kc, jnp.exp(rel))
        o = (jnp.einsum('bhij,bjhv->bihv', A, vc)
             + jnp.einsum('bihd,bhdv->bihv', qc * jnp.exp(b), S))
        b_last = b[:, -1]
        S = (jnp.exp(b_last)[..., None] * S
             + jnp.einsum('bjhd,bjhv->bhdv', kc * jnp.exp(b_last[:, None] - b), vc))
        return S, o

    S, o = lax.scan(step, s0.astype(jnp.float32), (chunks(q), chunks(k), chunks(v), chunks(log_a)))
    return jnp.moveaxis(o, 0, 1).reshape(B, T, H, DV), S


def layer_forward(x, pos, l, p, past):
    B, T, _ = x.shape
    h = rmsnorm(x, p['attn_norm'][l])
    z = h @ p['w_in'][l]
    qa, ka, va, qg, kg, vg, gg, ag, cq, ckv, kr = jnp.split(z, split_points(), axis=-1)

    ka_rows = ka.reshape(B, T, H_A, 2 * DK_A)
    va_rows = va.reshape(B, T, H_A, DV_A)
    if past is None:
        k_all, v_all, k_pos = ka_rows, va_rows, pos
    else:
        k_all = jnp.concatenate([past[0].astype(ka_rows.dtype), ka_rows], axis=1)
        v_all = jnp.concatenate([past[1].astype(va_rows.dtype), va_rows], axis=1)
        k_pos = jnp.concatenate([jnp.arange(past[0].shape[1], dtype=jnp.int32), pos])
    lam_init = 0.8 - 0.6 * math.exp(-0.3 * l)
    lq1, lk1, lq2, lk2 = p['diff_lambda'][l].astype(jnp.float32)
    lam = jnp.exp(jnp.sum(lq1 * lk1)) - jnp.exp(jnp.sum(lq2 * lk2)) + lam_init
    oa = diff_attention(qa.reshape(B, T, H_A, 2, DK_A), k_all.reshape(B, -1, H_A, 2, DK_A),
                        v_all, pos, k_pos, p['rel_bias'], lam)
    oa = rmsnorm(oa, p['diff_subln'][l]) * (1.0 - lam_init)

    qg = qg.reshape(B, T, H_B, DK_B) * (DK_B ** -0.5)
    kg = kg.reshape(B, T, H_B, DK_B)
    vg = vg.reshape(B, T, H_B, DV_B)
    log_a = jax.nn.log_sigmoid((ag @ p['w_gla_a2'][l] + p['b_gla_a'][l]).astype(jnp.float32)) / GATE_TAU
    log_a = log_a.reshape(B, T, H_B, DK_B)
    s0 = jnp.zeros((B, H_B, DK_B, DV_B), jnp.float32) if past is None else past[3]
    og, s_new = gla_recurrence(qg, kg, vg, log_a, s0)
    og = rmsnorm(og.astype(x.dtype), p['gla_norm'][l]) * jax.nn.silu(gg.reshape(B, T, H_B, DV_B))

    cq = rmsnorm(cq, p['mla_q_norm'][l])
    qc = (cq @ p['w_uq'][l]).reshape(B, T, H_C, NOPE_C + ROPE_C)
    q_nope = qc[..., :NOPE_C]
    q_rope = rope(qc[..., NOPE_C:], pos)
    c_rows = jnp.concatenate([rmsnorm(ckv, p['mla_kv_norm'][l]), rope(kr, pos)], axis=-1)
    c_all = c_rows if past is None else jnp.concatenate([past[2].astype(c_rows.dtype), c_rows], axis=1)
    q_lat = jnp.einsum('bthn,hcn->bthc', q_nope, p['w_uk'][l])
    o_lat = mla_attention(q_lat, q_rope, c_all[..., :KV_LORA], c_all[..., KV_LORA:], pos, k_pos)
    oc = jnp.einsum('bthc,hcv->bthv', o_lat, p['w_uv'][l])

    mix = jnp.concatenate([oa.reshape(B, T, -1), og.reshape(B, T, -1), oc.reshape(B, T, -1)], axis=-1)
    x = x + mix @ p['w_out'][l]

    h2 = rmsnorm(x, p['ffn_norm'][l])
    g = h2 @ p['w_gate'][l]
    u = h2 @ p['w_up'][l]
    prev = jnp.zeros((B, CONV_W - 1, D_FF), g.dtype) if past is None else past[4].astype(g.dtype)
    cat = jnp.concatenate([prev, g], axis=1)
    gc = p['conv_b'][l]
    for j in range(CONV_W):
        gc = gc + cat[:, j:j + T] * p['conv_w'][l][j]
    x = x + (jax.nn.silu(gc) * u) @ p['w_down'][l]
    new_buf = cat[:, T:]
    return x, (ka_rows, va_rows, c_rows, s_new.astype(x.dtype), new_buf)


def run_trunk(x, pos, p, past_fn):
    rows = []
    for l in range(DEPTH):
        x, st = layer_forward(x, pos, l, p, past_fn(l))
        rows.append(st)
    y = rmsnorm(x, p['final_norm'])
    new = [jnp.stack([r[i] for r in rows], axis=0) for i in range(5)]
    return y, new


def setup_inputs(seed: int = 0) -> dict:
    key = jax.random.key(seed)
    ks = jax.random.split(key, 32)
    f32 = jnp.float32
    n_pages = PAST_LEN // PAGE_SIZE
    n_used = DEC_BATCH * n_pages
    n_pool = n_used + n_used // 4

    def nrm(k, shape, scale):
        return jax.random.normal(k, shape, f32) * scale

    def gain(k, shape):
        return 1.0 + 0.05 * jax.random.normal(k, shape, f32)

    page_table = jax.random.permutation(ks[7], n_pool)[:n_used].reshape(DEC_BATCH, n_pages).astype(jnp.int32)
    return {
        'x_prompt': nrm(ks[0], (BATCH, SEQ, D_MODEL), 1.0),
        'x_sample': nrm(ks[1], (DEC_BATCH, DEC_SEQ, D_MODEL), 1.0),
        'cache_diff_k': nrm(ks[2], (DEPTH, n_pool, PAGE_SIZE, H_A, 2 * DK_A), 1.0),
        'cache_diff_v': nrm(ks[3], (DEPTH, n_pool, PAGE_SIZE, H_A, DV_A), 1.0),
        'cache_mla': nrm(ks[4], (DEPTH, n_pool, PAGE_SIZE, KV_LORA + ROPE_C), 1.0),
        'state_gla': nrm(ks[5], (DEPTH, DEC_BATCH, H_B, DK_B, DV_B), 0.5),
        'state_ffn_conv': nrm(ks[6], (DEPTH, DEC_BATCH, CONV_W - 1, D_FF), 1.0),
        'page_table': page_table,
        'attn_norm': gain(ks[8], (DEPTH, D_MODEL)),
        'w_in': nrm(ks[9], (DEPTH, D_MODEL, D_IN), D_MODEL ** -0.5),
        'diff_lambda': nrm(ks[10], (DEPTH, 4, DK_A), 0.1),
        'diff_subln': gain(ks[11], (DEPTH, DV_A)),
        'rel_bias': nrm(ks[12], (N_BUCKETS, H_A), 0.5),
        'w_gla_a2': nrm(ks[13], (DEPTH, GATE_RANK, H_B * DK_B), GATE_RANK ** -0.5),
        'b_gla_a': nrm(ks[14], (DEPTH, H_B * DK_B), 0.1),
        'gla_norm': gain(ks[15], (DEPTH, DV_B)),
        'mla_q_norm': gain(ks[16], (DEPTH, Q_LORA)),
        'w_uq': nrm(ks[17], (DEPTH, Q_LORA, H_C * (NOPE_C + ROPE_C)), Q_LORA ** -0.5),
        'mla_kv_norm': gain(ks[18], (DEPTH, KV_LORA)),
        'w_uk': nrm(ks[19], (DEPTH, H_C, KV_LORA, NOPE_C), KV_LORA ** -0.5),
        'w_uv': nrm(ks[20], (DEPTH, H_C, KV_LORA, DV_C), KV_LORA ** -0.5),
        'w_out': nrm(ks[21], (DEPTH, D_MIX, D_MODEL), D_MIX ** -0.5),
        'ffn_norm': gain(ks[22], (DEPTH, D_MODEL)),
        'w_gate': nrm(ks[23], (DEPTH, D_MODEL, D_FF), D_MODEL ** -0.5),
        'w_up': nrm(ks[24], (DEPTH, D_MODEL, D_FF), D_MODEL ** -0.5),
        'conv_w': nrm(ks[25], (DEPTH, CONV_W, D_FF), CONV_W ** -0.5),
        'conv_b': nrm(ks[26], (DEPTH, D_FF), 0.02),
        'w_down': nrm(ks[27], (DEPTH, D_FF, D_MODEL), D_FF ** -0.5),
        'final_norm': gain(ks[28], (D_MODEL,)),
    }


def reference(x_prompt, x_sample, cache_diff_k, cache_diff_v, cache_mla, state_gla, state_ffn_conv,
              page_table, attn_norm, w_in, diff_lambda, diff_subln, rel_bias, w_gla_a2, b_gla_a,
              gla_norm, mla_q_norm, w_uq, mla_kv_norm, w_uk, w_uv, w_out, ffn_norm, w_gate, w_up,
              conv_w, conv_b, w_down, final_norm):
    p = {'attn_norm': attn_norm, 'w_in': w_in, 'diff_lambda': diff_lambda, 'diff_subln': diff_subln,
         'rel_bias': rel_bias, 'w_gla_a2': w_gla_a2, 'b_gla_a': b_gla_a, 'gla_norm': gla_norm,
         'mla_q_norm': mla_q_norm, 'w_uq': w_uq, 'mla_kv_norm': mla_kv_norm, 'w_uk': w_uk, 'w_uv': w_uv,
         'w_out': w_out, 'ffn_norm': ffn_norm, 'w_gate': w_gate, 'w_up': w_up, 'conv_w': conv_w,
         'conv_b': conv_b, 'w_down': w_down, 'final_norm': final_norm}

    pos_p = jnp.arange(x_prompt.shape[1], dtype=jnp.int32)
    y_prompt, new_p = run_trunk(x_prompt, pos_p, p, lambda l: None)
    k_p, v_p, mla_p, gla_p, conv_p = new_p

    n_seq = page_table.shape[0]
    past_len = page_table.shape[1] * cache_diff_k.shape[2]
    pos_s = past_len + jnp.arange(x_sample.shape[1], dtype=jnp.int32)

    def sample_past(l):
        return (cache_diff_k[l, page_table].reshape(n_seq, -1, H_A, 2 * DK_A),
                cache_diff_v[l, page_table].reshape(n_seq, -1, H_A, DV_A),
                cache_mla[l, page_table].reshape(n_seq, -1, KV_LORA + ROPE_C),
                state_gla[l], state_ffn_conv[l])

    y_sample, new_s = run_trunk(x_sample, pos_s, p, sample_past)
    k_s, v_s, mla_s, gla_s, conv_s = new_s
    return (y_prompt, y_sample, k_p, v_p, mla_p, gla_p, conv_p, k_s, v_s, mla_s, gla_s, conv_s)
```

```python
import functools
import math

import jax
import jax.numpy as jnp
import numpy as np
from jax import lax
from jax.experimental import pallas as pl
from jax.experimental.pallas import tpu as pltpu

BF = jnp.bfloat16
F32 = jnp.float32

H_A, DK_A, DV_A = 4, 32, 64
H_B, DK_B, DV_B = 4, 64, 128
GATE_RANK, GATE_TAU, GLA_CHUNK = 16, 16.0, 64
H_C, Q_LORA, KV_LORA, NOPE_C, ROPE_C, DV_C = 4, 192, 128, 64, 32, 64
ROPE_THETA = 10000.0
N_BUCKETS, MAX_DISTANCE = 32, 128
NORM_EPS = 1e-6
CONV_W = 3

WA = H_A * 2 * DK_A
WV = H_A * DV_A
WK_B = H_B * DK_B
WV_B = H_B * DV_B
LANE = 128
VMEM_LIMIT = 56 * 1024 * 1024
ROW_TILE = 256
ATTN_TILE = 512
GLA_ROWS = 512
PAGES_PER_STEP = 8

_C_QA, _C_KA, _C_VA, _C_QG, _C_KG, _C_VG, _C_GG = 0, 256, 512, 768, 1024, 1280, 1792
_C_AG, _C_CQ, _C_CKV, _C_KRA, _C_KRB, _C_END = 2304, 2432, 2688, 2816, 2944, 3072

_NT = (((1,), (1,)), ((), ()))
_TN = (((0,), (0,)), ((), ()))


def _cparams(sem):
    return pltpu.CompilerParams(dimension_semantics=sem, vmem_limit_bytes=VMEM_LIMIT)


def _const_spec(shape):
    nd = len(shape)
    return pl.BlockSpec(shape, lambda *_: (0,) * nd, pipeline_mode=pl.Buffered(1))


def _inproj_kernel(x_ref, nw_ref, w_ref, w2_ref, b2_ref, mqn_ref, wuq_ref, wuk_ref, mkn_ref,
                   cos_ref, sin_ref,
                   qa_ref, kaf_ref, vaf_ref, kab_ref, vab_ref, qg_ref, kg_ref, vg_ref, gg_ref,
                   la_ref, qm_ref, cp_ref, mla_ref):
    x = x_ref[...]
    h = x * lax.rsqrt(jnp.mean(x * x, axis=-1, keepdims=True) + NORM_EPS) * nw_ref[...]
    z = jnp.dot(h.astype(BF), w_ref[...], preferred_element_type=F32)

    qa_ref[...] = (z[:, _C_QA:_C_KA] * (DK_A ** -0.5)).astype(qa_ref.dtype)
    ka = z[:, _C_KA:_C_VA]
    va = z[:, _C_VA:_C_QG]
    kaf_ref[...] = ka
    vaf_ref[...] = va
    kab_ref[...] = ka.astype(kab_ref.dtype)
    vab_ref[...] = va.astype(vab_ref.dtype)

    qg_ref[...] = z[:, _C_QG:_C_KG] * (DK_B ** -0.5)
    kg_ref[...] = z[:, _C_KG:_C_VG]
    vg_ref[...] = z[:, _C_VG:_C_GG].astype(vg_ref.dtype)
    gg_ref[...] = z[:, _C_GG:_C_AG]
    ag = z[:, _C_AG:_C_CQ]
    xg = jnp.dot(ag.astype(BF), w2_ref[...], preferred_element_type=F32) + b2_ref[...]
    la_ref[...] = (jnp.minimum(xg, 0.0) - jnp.log1p(jnp.exp(-jnp.abs(xg)))) * (1.0 / GATE_TAU)

    cos = cos_ref[...]
    sin = sin_ref[...]
    cq = z[:, _C_CQ:_C_CKV]
    cqn = cq * lax.rsqrt(jnp.sum(cq * cq, axis=-1, keepdims=True) * (1.0 / Q_LORA) + NORM_EPS) * mqn_ref[...]
    qc = jnp.dot(cqn.astype(BF), wuq_ref[...], preferred_element_type=F32)
    scale_c = (NOPE_C + ROPE_C) ** -0.5
    q_rope = (qc[:, 256:384] * cos + qc[:, 384:512] * sin) * scale_c
    q_lat = jnp.dot(qc[:, 0:256].astype(BF), wuk_ref[...], preferred_element_type=F32) * scale_c
    lane = lax.broadcasted_iota(jnp.int32, q_rope.shape, 1)
    for hh in range(H_C):
        qm_ref[hh, :, 0:KV_LORA] = q_lat[:, hh * KV_LORA:(hh + 1) * KV_LORA].astype(qm_ref.dtype)
        qm_ref[hh, :, KV_LORA:2 * KV_LORA] = jnp.where(lane // ROPE_C == hh, q_rope, 0.0).astype(qm_ref.dtype)
    ckv = z[:, _C_CKV:_C_KRA]
    ckvn = ckv * lax.rsqrt(jnp.mean(ckv * ckv, axis=-1, keepdims=True) + NORM_EPS) * mkn_ref[...]
    krr = z[:, _C_KRA:_C_KRB] * cos + z[:, _C_KRB:_C_END] * sin
    cp_ref[:, 0:KV_LORA] = ckvn.astype(cp_ref.dtype)
    cp_ref[:, KV_LORA:2 * KV_LORA] = krr.astype(cp_ref.dtype)
    mla_ref[:, 0:KV_LORA] = ckvn
    mla_ref[:, KV_LORA:KV_LORA + ROPE_C] = krr[:, 0:ROPE_C]


def _inproj(x, lw, cos_t, sin_t, tm, act_dtype):
    n, d = x.shape
    grid = (n // tm,)
    row = lambda w: pl.BlockSpec((tm, w), lambda i: (i, 0))
    in_specs = [
        row(d), _const_spec((1, d)), _const_spec(lw['w_in'].shape), _const_spec(lw['w2'].shape),
        _const_spec((1, WK_B)), _const_spec((1, 256)), _const_spec(lw['wuq'].shape),
        _const_spec(lw['wuk'].shape), _const_spec((1, KV_LORA)), row(LANE), row(LANE),
    ]
    outs = [
        (WA, act_dtype), (WA, F32), (WV, F32), (WA, act_dtype), (WV, act_dtype),
        (WK_B, F32), (WK_B, F32), (WV_B, act_dtype), (WV_B, F32), (WK_B, F32),
    ]
    out_shape = [jax.ShapeDtypeStruct((n, w), dt) for w, dt in outs]
    out_specs = [row(w) for w, _ in outs]
    out_shape += [jax.ShapeDtypeStruct((H_C, n, 256), act_dtype), jax.ShapeDtypeStruct((n, 256), act_dtype),
                  jax.ShapeDtypeStruct((n, KV_LORA + ROPE_C), F32)]
    out_specs += [pl.BlockSpec((H_C, tm, 256), lambda i: (0, i, 0)), row(256),
                  pl.BlockSpec((tm, KV_LORA + ROPE_C), lambda i: (i, 0))]
    return pl.pallas_call(
        _inproj_kernel, grid=grid, in_specs=in_specs, out_specs=out_specs, out_shape=out_shape,
        compiler_params=_cparams(("parallel",)), name="inproj",
    )(x, lw['attn_norm'], lw['w_in'], lw['w2'], lw['b2'], lw['mqn'], lw['wuq'], lw['wuk'], lw['mkn'],
      cos_t, sin_t)


def _diff_attn_kernel(qi_ref, ki_ref, q_ref, k_ref, v_ref, bias_ref, gain_ref, lam_ref, o_ref,
                      qs_ref, m_ref, l_ref, acc_ref, *, tq):
    p = pl.program_id(1)
    qi = qi_ref[p]
    ki = ki_ref[p]
    lane = lax.broadcasted_iota(jnp.int32, (tq, WA), 1)

    @pl.when(ki == 0)
    def _init():
        q = q_ref[...]
        zero = jnp.zeros_like(q)
        for hm in range(2 * H_A):
            qs_ref[hm * tq:(hm + 1) * tq, :] = jnp.where(lane // DK_A == hm, q, zero)
        m_ref[...] = jnp.full(m_ref.shape, -jnp.inf, F32)
        l_ref[...] = jnp.zeros(l_ref.shape, F32)
        acc_ref[...] = jnp.zeros(acc_ref.shape, F32)

    def step(with_bias):
        k = k_ref[...]
        v = v_ref[...]
        for hh in range(H_A):
            r0, r1 = 2 * hh * tq, 2 * (hh + 1) * tq
            s = lax.dot_general(qs_ref[r0:r1, :], k, _NT, preferred_element_type=F32)
            if with_bias:
                bh = bias_ref[hh]
                s = s + jnp.concatenate([bh, bh], axis=0)
            m_old = m_ref[r0:r1, :]
            m_new = jnp.maximum(m_old, jnp.max(s, axis=-1, keepdims=True))
            alpha = jnp.exp(m_old - m_new)
            pr = jnp.exp(s - m_new)
            l_ref[r0:r1, :] = alpha * l_ref[r0:r1, :] + jnp.sum(pr, axis=-1, keepdims=True)
            acc_ref[r0:r1, :] = alpha * acc_ref[r0:r1, :] + jnp.dot(pr.astype(BF), v, preferred_element_type=F32)
            m_ref[r0:r1, :] = m_new

    pl.when(qi - ki <= 1)(lambda: step(True))
    pl.when(qi - ki > 1)(lambda: step(False))

    @pl.when(ki == qi)
    def _fin():
        lam = lam_ref[...]
        out = jnp.zeros((tq, WV), F32)
        for hh in range(H_A):
            a0, a1, a2 = 2 * hh * tq, (2 * hh + 1) * tq, (2 * hh + 2) * tq
            dh = acc_ref[a0:a1, :] / l_ref[a0:a1, :] - lam * (acc_ref[a1:a2, :] / l_ref[a1:a2, :])
            hmask = lane // DV_A == hh
            ms = jnp.sum(jnp.where(hmask, dh * dh, 0.0), axis=-1, keepdims=True) * (1.0 / DV_A)
            out = jnp.where(hmask, dh * lax.rsqrt(ms + NORM_EPS), out)
        o_ref[...] = (out * gain_ref[...]).astype(o_ref.dtype)


def _causal_pairs(nq):
    qi = np.concatenate([np.full(i + 1, i, np.int32) for i in range(nq)])
    ki = np.concatenate([np.arange(i + 1, dtype=np.int32) for i in range(nq)])
    return jnp.asarray(qi), jnp.asarray(ki)


def _diff_attn(q, k, v, bias, gain, lam, tq):
    b, t, _ = q.shape
    nq = t // tq
    qi, ki = _causal_pairs(nq)
    grid_spec = pltpu.PrefetchScalarGridSpec(
        num_scalar_prefetch=2, grid=(b, int(qi.shape[0])),
        in_specs=[
            pl.BlockSpec((None, tq, WA), lambda bb, p, qi, ki: (bb, qi[p], 0)),
            pl.BlockSpec((None, tq, WA), lambda bb, p, qi, ki: (bb, ki[p], 0)),
            pl.BlockSpec((None, tq, WV), lambda bb, p, qi, ki: (bb, ki[p], 0)),
            pl.BlockSpec((None, H_A, tq, tq), lambda bb, p, qi, ki: (jnp.minimum(qi[p] - ki[p], 1), 0, 0, 0)),
            pl.BlockSpec((1, WV), lambda bb, p, qi, ki: (0, 0)),
            pl.BlockSpec((1, 1), lambda bb, p, qi, ki: (0, 0)),
        ],
        out_specs=pl.BlockSpec((None, tq, WV), lambda bb, p, qi, ki: (bb, qi[p], 0)),
        scratch_shapes=[
            pltpu.VMEM((2 * H_A * tq, WA), BF), pltpu.VMEM((2 * H_A * tq, 1), F32),
            pltpu.VMEM((2 * H_A * tq, 1), F32), pltpu.VMEM((2 * H_A * tq, WV), F32),
        ],
    )
    return pl.pallas_call(
        functools.partial(_diff_attn_kernel, tq=tq), grid_spec=grid_spec,
        out_shape=jax.ShapeDtypeStruct((b, t, WV), BF),
        compiler_params=_cparams(("parallel", "arbitrary")), name="diff_attn",
    )(qi, ki, q, k, v, bias, gain, lam)


def _mla_attn_kernel(qi_ref, ki_ref, qm_ref, c_ref, wuv_ref, o_ref, m_ref, l_ref, acc_ref, *, tq):
    p = pl.program_id(1)
    qi = qi_ref[p]
    ki = ki_ref[p]

    @pl.when(ki == 0)
    def _init():
        m_ref[...] = jnp.full(m_ref.shape, -jnp.inf, F32)
        l_ref[...] = jnp.zeros(l_ref.shape, F32)
        acc_ref[...] = jnp.zeros(acc_ref.shape, F32)

    def step(masked):
        c = c_ref[...]
        cv = c_ref[:, 0:KV_LORA]
        if masked:
            causal = (lax.broadcasted_iota(jnp.int32, (tq, tq), 0) >= lax.broadcasted_iota(jnp.int32, (tq, tq), 1))
        for hh in range(H_C):
            r0, r1 = hh * tq, (hh + 1) * tq
            s = lax.dot_general(qm_ref[hh], c, _NT, preferred_element_type=F32)
            if masked:
                s = jnp.where(causal, s, -jnp.inf)
            m_old = m_ref[r0:r1, :]
            m_new = jnp.maximum(m_old, jnp.max(s, axis=-1, keepdims=True))
            alpha = jnp.exp(m_old - m_new)
            pr = jnp.exp(s - m_new)
            l_ref[r0:r1, :] = alpha * l_ref[r0:r1, :] + jnp.sum(pr, axis=-1, keepdims=True)
            acc_ref[r0:r1, :] = alpha * acc_ref[r0:r1, :] + jnp.dot(pr.astype(BF), cv, preferred_element_type=F32)
            m_ref[r0:r1, :] = m_new

    pl.when(ki == qi)(lambda: step(True))
    pl.when(ki != qi)(lambda: step(False))

    @pl.when(ki == qi)
    def _fin():
        out = jnp.zeros((tq, H_C * DV_C), F32)
        for hh in range(H_C):
            r0, r1 = hh * tq, (hh + 1) * tq
            o_lat = acc_ref[r0:r1, :] / l_ref[r0:r1, :]
            out = out + jnp.dot(o_lat.astype(BF), wuv_ref[hh], preferred_element_type=F32)
        o_ref[...] = out.astype(o_ref.dtype)


def _mla_attn(qm, cpad, wuv, tq):
    b, t, _ = cpad.shape
    nq = t // tq
    qi, ki = _causal_pairs(nq)
    grid_spec = pltpu.PrefetchScalarGridSpec(
        num_scalar_prefetch=2, grid=(b, int(qi.shape[0])),
        in_specs=[
            pl.BlockSpec((H_C, None, tq, 256), lambda bb, p, qi, ki: (0, bb, qi[p], 0)),
            pl.BlockSpec((None, tq, 256), lambda bb, p, qi, ki: (bb, ki[p], 0)),
            pl.BlockSpec((H_C, KV_LORA, H_C * DV_C), lambda bb, p, qi, ki: (0, 0, 0)),
        ],
        out_specs=pl.BlockSpec((None, tq, H_C * DV_C), lambda bb, p, qi, ki: (bb, qi[p], 0)),
        scratch_shapes=[
            pltpu.VMEM((H_C * tq, 1), F32), pltpu.VMEM((H_C * tq, 1), F32),
            pltpu.VMEM((H_C * tq, KV_LORA), F32),
        ],
    )
    return pl.pallas_call(
        functools.partial(_mla_attn_kernel, tq=tq), grid_spec=grid_spec,
        out_shape=jax.ShapeDtypeStruct((b, t, H_C * DV_C), BF),
        compiler_params=_cparams(("parallel", "arbitrary")), name="mla_attn",
    )(qi, ki, qm, cpad, wuv)


def _gla_kernel(q_ref, k_ref, v_ref, g_ref, gate_ref, s0_ref, nw_ref, o_ref, so_ref, st_ref,
                *, chunk, sub, nchunk):
    t = pl.program_id(1)

    @pl.when(t == 0)
    def _init():
        st_ref[...] = s0_ref[...]

    C, c = chunk, sub
    tri = (lax.broadcasted_iota(jnp.int32, (C, C), 0) >= lax.broadcasted_iota(jnp.int32, (C, C), 1)).astype(F32)
    ind = (lax.broadcasted_iota(jnp.int32, (WK_B, WV_B), 0) // DK_B
           == lax.broadcasted_iota(jnp.int32, (WK_B, WV_B), 1) // DV_B).astype(BF)
    bdmask = (lax.broadcasted_iota(jnp.int32, (WV_B, WK_B), 0) // DV_B
              == lax.broadcasted_iota(jnp.int32, (WV_B, WK_B), 1) // DK_B)
    lane_k = lax.broadcasted_iota(jnp.int32, (c, WK_B), 1)
    row_c = lax.broadcasted_iota(jnp.int32, (c, WK_B), 0)
    nw = nw_ref[...]

    def do_chunk(n, carry):
        r = pl.multiple_of(n * C, C)
        q = q_ref[pl.ds(r, C), :]
        k = k_ref[pl.ds(r, C), :]
        v = v_ref[pl.ds(r, C), :].astype(BF)
        g = g_ref[pl.ds(r, C), :]
        b = jnp.dot(tri, g, preferred_element_type=F32, precision=lax.Precision.HIGHEST)
        b_last = b[C - 1:C, :]
        st = st_ref[...]
        qt = (q * jnp.exp(b)).astype(BF)
        o = lax.dot_general(qt, st.astype(BF), _NT, preferred_element_type=F32)
        kt = (k * jnp.exp(b_last - b)).astype(BF)
        u = lax.dot_general(v, kt, _TN, preferred_element_type=F32)
        st_ref[...] = st * jnp.exp(b_last) + jnp.where(bdmask, u, 0.0)
        rows = []
        for blk in range(C // c):
            lo, hi = blk * c, (blk + 1) * c
            b_i, q_i, k_i = b[lo:hi], q[lo:hi], k[lo:hi]
            v_i = v[lo:hi].astype(F32)
            es = []
            for j in range(c):
                e = jnp.exp(jnp.minimum(b_i - b_i[j:j + 1], 0.0)) * q_i * k_i[j:j + 1]
                es.append(jnp.where(row_c >= j, e, 0.0))
            rr = jnp.dot(jnp.concatenate(es, axis=0).astype(BF), ind, preferred_element_type=F32)
            od = rr[0:c] * v_i[0:1]
            for j in range(1, c):
                od = od + rr[j * c:(j + 1) * c] * v_i[j:j + 1]
            if blk > 0:
                b_s = b[lo - 1:lo]
                qh = q_i * jnp.exp(b_i - b_s)
                kh = (k[0:lo] * jnp.exp(b_s - b[0:lo])).astype(BF)
                qs = jnp.concatenate([jnp.where(lane_k // DK_B == hh, qh, 0.0) for hh in range(H_B)], axis=0)
                a = lax.dot_general(qs.astype(BF), kh, _NT, preferred_element_type=F32)
                pv = jnp.dot(a.astype(BF), v[0:lo], preferred_element_type=F32)
                od = od + jnp.concatenate(
                    [pv[hh * c:(hh + 1) * c, hh * DV_B:(hh + 1) * DV_B] for hh in range(H_B)], axis=1)
            rows.append(od)
        o = o + (jnp.concatenate(rows, axis=0) if len(rows) > 1 else rows[0])
        gate = gate_ref[pl.ds(r, C), :]
        outs = []
        for hh in range(H_B):
            oh = o[:, hh * DV_B:(hh + 1) * DV_B]
            outs.append(oh * lax.rsqrt(jnp.mean(oh * oh, axis=-1, keepdims=True) + NORM_EPS) * nw)
        y = jnp.concatenate(outs, axis=1) * (gate * jax.nn.sigmoid(gate))
        o_ref[pl.ds(r, C), :] = y.astype(o_ref.dtype)
        return carry

    if nchunk == 1:
        do_chunk(0, 0)
    else:
        lax.fori_loop(0, nchunk, do_chunk, 0)

    @pl.when(t == pl.num_programs(1) - 1)
    def _fin():
        so_ref[...] = st_ref[...]


def _gla(q, k, v, g, gate, s0t, nw, rows, chunk, sub, out_dtype):
    b, t, _ = q.shape
    nt = t // rows
    blk = lambda w: pl.BlockSpec((None, rows, w), lambda bb, tt: (bb, tt, 0))
    st_spec = pl.BlockSpec((None, WV_B, WK_B), lambda bb, tt: (bb, 0, 0))
    return pl.pallas_call(
        functools.partial(_gla_kernel, chunk=chunk, sub=sub, nchunk=rows // chunk),
        grid=(b, nt),
        in_specs=[blk(WK_B), blk(WK_B), blk(WV_B), blk(WK_B), blk(WV_B), st_spec,
                  pl.BlockSpec((1, DV_B), lambda bb, tt: (0, 0))],
        out_specs=[blk(WV_B), st_spec],
        out_shape=[jax.ShapeDtypeStruct((b, t, WV_B), out_dtype), jax.ShapeDtypeStruct((b, WV_B, WK_B), F32)],
        scratch_shapes=[pltpu.VMEM((WV_B, WK_B), F32)],
        compiler_params=_cparams(("parallel", "arbitrary")), name="gla",
    )(q, k, v, g, gate, s0t, nw)


def _ffn_kernel(*refs, tm, per_seq_rows, with_final):
    it = iter(refs)
    x_ref, oa_ref, og_ref, oc_ref, wo_ref, fnw_ref, wg_ref, wu_ref, cw_ref, cb_ref, wd_ref = (next(it) for _ in range(11))
    prev_ref = next(it) if per_seq_rows else None
    finw_ref = next(it) if with_final else None
    xo_ref, gt_ref = next(it), next(it)
    y_ref = next(it) if with_final else None
    carry_ref = None if per_seq_rows else next(it)

    x = x_ref[...]
    mix = (jnp.dot(oa_ref[...].astype(BF), wo_ref[0:WV, :], preferred_element_type=F32)
           + jnp.dot(og_ref[...].astype(BF), wo_ref[WV:WV + WV_B, :], preferred_element_type=F32)
           + jnp.dot(oc_ref[...].astype(BF), wo_ref[WV + WV_B:, :], preferred_element_type=F32))
    xm = x + mix
    hb = (xm * lax.rsqrt(jnp.mean(xm * xm, axis=-1, keepdims=True) + NORM_EPS) * fnw_ref[...]).astype(BF)
    g = jnp.dot(hb, wg_ref[...], preferred_element_type=F32)
    u = jnp.dot(hb, wu_ref[...], preferred_element_type=F32)
    row = lax.broadcasted_iota(jnp.int32, g.shape, 0)
    g1 = pltpu.roll(g, 1, axis=0)
    g2 = pltpu.roll(g, 2, axis=0)
    if per_seq_rows:
        rm = row % per_seq_rows
        p0, p1 = prev_ref[0], prev_ref[1]
        g1 = jnp.where(rm == 0, p1, g1)
        g2 = jnp.where(rm == 0, p0, jnp.where(rm == 1, p1, g2))
        gt_ref[...] = g
    else:
        @pl.when(pl.program_id(1) == 0)
        def _zero():
            carry_ref[...] = jnp.zeros(carry_ref.shape, F32)
        p0, p1 = carry_ref[6:7, :], carry_ref[7:8, :]
        g1 = jnp.where(row == 0, p1, g1)
        g2 = jnp.where(row == 0, p0, jnp.where(row == 1, p1, g2))
        tail = g[tm - 8:tm, :]
        carry_ref[...] = tail
        gt_ref[...] = tail
    gc = cb_ref[...] + g2 * cw_ref[0:1, :] + g1 * cw_ref[1:2, :] + g * cw_ref[2:3, :]
    act = (gc * jax.nn.sigmoid(gc) * u).astype(BF)
    xo = xm + jnp.dot(act, wd_ref[...], preferred_element_type=F32)
    xo_ref[...] = xo
    if with_final:
        y_ref[...] = xo * lax.rsqrt(jnp.mean(xo * xo, axis=-1, keepdims=True) + NORM_EPS) * finw_ref[...]


def _ffn(x, oa, og, oc, lw, tm, prev=None, final_w=None):
    b, t, d = x.shape
    f = lw['w_gate'].shape[1]
    nt = t // tm
    per_seq = prev is not None
    blk = lambda w: pl.BlockSpec((None, tm, w), lambda bb, tt: (bb, tt, 0))
    in_specs = [blk(d), blk(WV), blk(WV_B), blk(H_C * DV_C), _const_spec(lw['w_out'].shape), _const_spec((1, d)),
                _const_spec(lw['w_gate'].shape), _const_spec(lw['w_up'].shape), _const_spec((CONV_W, f)),
                _const_spec((1, f)), _const_spec(lw['w_down'].shape)]
    args = [x, oa, og, oc, lw['w_out'], lw['ffn_norm'], lw['w_gate'], lw['w_up'], lw['conv_w'], lw['conv_b'],
            lw['w_down']]
    if per_seq:
        in_specs.append(pl.BlockSpec((2, None, tm, f), lambda bb, tt: (0, bb, tt, 0)))
        args.append(prev)
    if final_w is not None:
        in_specs.append(_const_spec((1, d)))
        args.append(final_w)
    gr = tm if per_seq else 8
    out_shape = [jax.ShapeDtypeStruct((b, t, d), F32), jax.ShapeDtypeStruct((b, nt, gr, f), F32)]
    out_specs = [blk(d), pl.BlockSpec((None, None, gr, f), lambda bb, tt: (bb, tt, 0, 0))]
    if final_w is not None:
        out_shape.append(jax.ShapeDtypeStruct((b, t, d), F32))
        out_specs.append(blk(d))
    scratch = [] if per_seq else [pltpu.VMEM((8, f), F32)]
    return pl.pallas_call(
        functools.partial(_ffn_kernel, tm=tm, per_seq_rows=8 if per_seq else 0, with_final=final_w is not None),
        grid=(b, nt), in_specs=in_specs, out_specs=out_specs, out_shape=out_shape, scratch_shapes=scratch,
        compiler_params=_cparams(("parallel", "arbitrary")), name="ffn",
    )(*args)


def _dec_attn_kernel(*refs, g_pages, t_new):
    pt_ref = refs[0]
    (qa_ref, qm_ref, kn_ref, vn_ref, cn_ref, bl_ref, bn_ref, gain_ref, lam_ref, wuv_ref) = refs[1:11]
    k_refs = refs[11:11 + g_pages]
    v_refs = refs[11 + g_pages:11 + 2 * g_pages]
    c_refs = refs[11 + 2 * g_pages:11 + 3 * g_pages]
    oa_ref, oc_ref = refs[11 + 3 * g_pages:13 + 3 * g_pages]
    (qs_ref, qms_ref, kt_ref, vt_ref, ct_ref, ma_ref, la_ref, acca_ref, mc_ref, lc_ref, accc_ref) = refs[13 + 3 * g_pages:]
    del pt_ref
    j = pl.program_id(1)
    nj = pl.num_programs(1)
    ra = 2 * H_A * t_new
    rc = H_C * t_new
    page = k_refs[0].shape[-1]

    @pl.when(j == 0)
    def _init():
        q = qa_ref[...]
        lane = lax.broadcasted_iota(jnp.int32, q.shape, 1)
        qs = jnp.concatenate([jnp.where(lane // DK_A == hm, q, 0.0) for hm in range(2 * H_A)], axis=0)
        qs_ref[...] = qs.astype(BF)
        qms_ref[...] = qm_ref[...].reshape(rc, 256).astype(BF)
        ma_ref[...] = jnp.full(ma_ref.shape, -jnp.inf, F32)
        la_ref[...] = jnp.zeros(la_ref.shape, F32)
        acca_ref[...] = jnp.zeros(acca_ref.shape, F32)
        mc_ref[...] = jnp.full(mc_ref.shape, -jnp.inf, F32)
        lc_ref[...] = jnp.zeros(lc_ref.shape, F32)
        accc_ref[...] = jnp.zeros(accc_ref.shape, F32)

    def online(s, m_ref, l_ref, acc_ref, pv):
        m_old = m_ref[...]
        m_new = jnp.maximum(m_old, jnp.max(s, axis=-1, keepdims=True))
        alpha = jnp.exp(m_old - m_new)
        pr = jnp.exp(s - m_new)
        l_ref[...] = alpha * l_ref[...] + jnp.sum(pr, axis=-1, keepdims=True)
        acc_ref[...] = alpha * acc_ref[...] + pv(pr.astype(BF))
        m_ref[...] = m_new

    for gg in range(g_pages):
        cols = slice(gg * page, (gg + 1) * page)
        kt_ref[:, cols] = k_refs[gg][...].astype(BF)
        vt_ref[:, cols] = v_refs[gg][...].astype(BF)
        cg = c_refs[gg][...]
        ct_ref[0:KV_LORA, cols] = cg[0:KV_LORA, :].astype(BF)
        kr = cg[KV_LORA:KV_LORA + ROPE_C, :].astype(BF)
        for hh in range(H_C):
            ct_ref[KV_LORA + hh * ROPE_C:KV_LORA + (hh + 1) * ROPE_C, cols] = kr

    def past(with_bias):
        s = jnp.dot(qs_ref[...], kt_ref[...], preferred_element_type=F32)
        if with_bias:
            s = s + bl_ref[...]
        online(s, ma_ref, la_ref, acca_ref,
               lambda pr: lax.dot_general(pr, vt_ref[...], _NT, preferred_element_type=F32))
        sc = jnp.dot(qms_ref[...], ct_ref[...], preferred_element_type=F32)
        online(sc, mc_ref, lc_ref, accc_ref,
               lambda pr: lax.dot_general(pr, ct_ref[0:KV_LORA, :], _NT, preferred_element_type=F32))

    pl.when(j == nj - 1)(lambda: past(True))
    pl.when(j != nj - 1)(lambda: past(False))

    @pl.when(j == nj - 1)
    def _fin():
        pad = page - t_new
        kn = jnp.concatenate([kn_ref[...], jnp.zeros((pad, WA), F32)], axis=0).astype(BF)
        vn = jnp.concatenate([vn_ref[...], jnp.zeros((pad, WV), F32)], axis=0).astype(BF)
        cn = jnp.concatenate([cn_ref[...], jnp.zeros((pad, 256), F32)], axis=0).astype(BF)
        s = lax.dot_general(qs_ref[...], kn, _NT, preferred_element_type=F32) + bn_ref[...]
        online(s, ma_ref, la_ref, acca_ref, lambda pr: jnp.dot(pr, vn, preferred_element_type=F32))
        tok = lax.broadcasted_iota(jnp.int32, (rc, page), 0) % t_new
        col = lax.broadcasted_iota(jnp.int32, (rc, page), 1)
        sc = lax.dot_general(qms_ref[...], cn, _NT, preferred_element_type=F32)
        sc = jnp.where(col <= tok, sc, -jnp.inf)
        online(sc, mc_ref, lc_ref, accc_ref, lambda pr: jnp.dot(pr, cn[:, 0:KV_LORA], preferred_element_type=F32))

        lam = lam_ref[...]
        lane = lax.broadcasted_iota(jnp.int32, (t_new, WV), 1)
        o = acca_ref[...] / la_ref[...]
        out = jnp.zeros((t_new, WV), F32)
        for hh in range(H_A):
            a0 = 2 * hh * t_new
            dh = o[a0:a0 + t_new] - lam * o[a0 + t_new:a0 + 2 * t_new]
            hmask = lane // DV_A == hh
            ms = jnp.sum(jnp.where(hmask, dh * dh, 0.0), axis=-1, keepdims=True) * (1.0 / DV_A)
            out = jnp.where(hmask, dh * lax.rsqrt(ms + NORM_EPS), out)
        oa_ref[...] = out * gain_ref[...]
        ol = (accc_ref[...] / lc_ref[...]).astype(BF)
        oc = jnp.zeros((t_new, H_C * DV_C), F32)
        for hh in range(H_C):
            oc = oc + jnp.dot(ol[hh * t_new:(hh + 1) * t_new], wuv_ref[hh], preferred_element_type=F32)
        oc_ref[...] = oc


def _dec_attn(layer, page_flat, n_pages, qa, qm, kn, vn, cn, bias_last, bias_new, gain, lam, wuv,
              kt_cache, vt_cache, ct_cache, g_pages):
    b, t_new, _ = qa.shape
    page = kt_cache.shape[-1]
    ng = n_pages // g_pages
    ra, rc = 2 * H_A * t_new, H_C * t_new
    seq = lambda w: pl.BlockSpec((None, t_new, w), lambda bb, jj, pt: (bb, 0, 0))
    const = lambda shape: pl.BlockSpec(shape, lambda bb, jj, pt: (0,) * len(shape))

    def page_spec(rows, gg):
        return pl.BlockSpec((None, None, rows, page),
                            lambda bb, jj, pt: (layer, pt[bb * n_pages + jj * g_pages + gg], 0, 0))

    in_specs = [seq(WA), pl.BlockSpec((H_C, None, t_new, 256), lambda bb, jj, pt: (0, bb, 0, 0)),
                seq(WA), seq(WV), seq(256), const((ra, g_pages * page)), const((ra, page)),
                const((1, WV)), const((1, 1)), const((H_C, KV_LORA, H_C * DV_C))]
    in_specs += [page_spec(WA, gg) for gg in range(g_pages)]
    in_specs += [page_spec(WV, gg) for gg in range(g_pages)]
    in_specs += [page_spec(KV_LORA + ROPE_C, gg) for gg in range(g_pages)]
    grid_spec = pltpu.PrefetchScalarGridSpec(
        num_scalar_prefetch=1, grid=(b, ng), in_specs=in_specs,
        out_specs=[seq(WV), seq(H_C * DV_C)],
        scratch_shapes=[
            pltpu.VMEM((ra, WA), BF), pltpu.VMEM((rc, 256), BF),
            pltpu.VMEM((WA, g_pages * page), BF), pltpu.VMEM((WV, g_pages * page), BF),
            pltpu.VMEM((256, g_pages * page), BF),
            pltpu.VMEM((ra, 1), F32), pltpu.VMEM((ra, 1), F32), pltpu.VMEM((ra, WV), F32),
            pltpu.VMEM((rc, 1), F32), pltpu.VMEM((rc, 1), F32), pltpu.VMEM((rc, KV_LORA), F32),
        ],
    )
    return pl.pallas_call(
        functools.partial(_dec_attn_kernel, g_pages=g_pages, t_new=t_new), grid_spec=grid_spec,
        out_shape=[jax.ShapeDtypeStruct((b, t_new, WV), F32), jax.ShapeDtypeStruct((b, t_new, H_C * DV_C), F32)],
        compiler_params=_cparams(("parallel", "arbitrary")), name="dec_attn",
    )(page_flat, qa, qm, kn, vn, cn, bias_last, bias_new, gain, lam, wuv,
      *([kt_cache] * g_pages), *([vt_cache] * g_pages), *([ct_cache] * g_pages))


def _split_points():
    sizes = [WA, WA, WV, WK_B, WK_B, WV_B, WV_B, GATE_RANK, Q_LORA, KV_LORA, ROPE_C]
    return [int(s) for s in np.cumsum(sizes)]


def _pad_cols(a, w):
    return jnp.pad(a, ((0, 0), (0, w - a.shape[1])))


def _prep_layer(l, p):
    sp = _split_points()
    w_in = p['w_in'][l]
    half = ROPE_C // 2
    swap = np.concatenate([np.arange(half, ROPE_C), np.arange(0, half)])
    kr = w_in[:, sp[9]:sp[10]]
    w_cat = jnp.concatenate([
        w_in[:, :sp[6]], _pad_cols(w_in[:, sp[6]:sp[7]], LANE), _pad_cols(w_in[:, sp[7]:sp[8]], 256),
        w_in[:, sp[8]:sp[9]], jnp.tile(kr, (1, H_C)), jnp.tile(kr[:, swap], (1, H_C))], axis=1)
    assert w_cat.shape[1] == _C_END
    w_uq = p['w_uq'][l].reshape(Q_LORA, H_C, NOPE_C + ROPE_C)
    rope_cols = w_uq[:, :, NOPE_C:]
    wuq = jnp.concatenate([w_uq[:, :, :NOPE_C].reshape(Q_LORA, -1), rope_cols.reshape(Q_LORA, -1),
                           rope_cols[:, :, swap].reshape(Q_LORA, -1)], axis=1)
    wuq = jnp.pad(wuq, ((0, 256 - Q_LORA), (0, 0)))
    eye = jnp.eye(H_C, dtype=F32)
    wuk = jnp.einsum('hcn,hg->hngc', p['w_uk'][l], eye).reshape(H_C * NOPE_C, H_C * KV_LORA)
    wuv = jnp.einsum('hcv,hg->hcgv', p['w_uv'][l], eye).reshape(H_C, KV_LORA, H_C * DV_C)
    lam_init = 0.8 - 0.6 * math.exp(-0.3 * l)
    lq1, lk1, lq2, lk2 = p['diff_lambda'][l].astype(F32)
    lam = jnp.exp(jnp.sum(lq1 * lk1)) - jnp.exp(jnp.sum(lq2 * lk2)) + lam_init
    return {
        'attn_norm': p['attn_norm'][l][None, :], 'w_in': w_cat.astype(BF),
        'w2': jnp.pad(p['w_gla_a2'][l], ((0, LANE - GATE_RANK), (0, 0))).astype(BF),
        'b2': p['b_gla_a'][l][None, :], 'mqn': _pad_cols(p['mla_q_norm'][l][None, :], 256),
        'wuq': wuq.astype(BF), 'wuk': wuk.astype(BF), 'mkn': p['mla_kv_norm'][l][None, :],
        'wuv': wuv.astype(BF), 'gain': jnp.tile(p['diff_subln'][l] * (1.0 - lam_init), H_A)[None, :],
        'lam': lam.reshape(1, 1), 'gla_norm': p['gla_norm'][l][None, :],
        'w_out': p['w_out'][l].astype(BF), 'ffn_norm': p['ffn_norm'][l][None, :],
        'w_gate': p['w_gate'][l].astype(BF), 'w_up': p['w_up'][l].astype(BF),
        'conv_w': p['conv_w'][l], 'conv_b': p['conv_b'][l][None, :], 'w_down': p['w_down'][l].astype(BF),
    }


def _rope_tables(pos):
    half = ROPE_C // 2
    inv = ROPE_THETA ** (-jnp.arange(half, dtype=F32) / half)
    ang = pos.astype(F32)[:, None] * inv
    cos, sin = jnp.cos(ang), jnp.sin(ang)
    cos_t = jnp.tile(jnp.concatenate([cos, cos], axis=1), (1, LANE // ROPE_C))
    sin_t = jnp.tile(jnp.concatenate([-sin, sin], axis=1), (1, LANE // ROPE_C))
    return cos_t, sin_t


def _t5_bucket(rel):
    n = jnp.maximum(rel, 0)
    max_exact = N_BUCKETS // 2
    nf = jnp.maximum(n, max_exact).astype(F32)
    large = max_exact + (jnp.log(nf / max_exact) / math.log(MAX_DISTANCE / max_exact)
                         * (N_BUCKETS - max_exact)).astype(jnp.int32)
    large = jnp.minimum(large, N_BUCKETS - 1)
    return jnp.where(n < max_exact, n, large)


def _rel_bias_tile(rel_table, rel):
    shifted = rel_table - rel_table[N_BUCKETS - 1]
    bias = jnp.moveaxis(shifted[_t5_bucket(rel)].astype(F32), -1, 0)
    return jnp.where(rel >= 0, bias, -jnp.inf)


def _prompt_trunk(x, p, lws):
    b, t, d = x.shape
    n = b * t
    tm = min(ROW_TILE, t)
    tq = min(ATTN_TILE, t)
    assert tq >= MAX_DISTANCE and t % tq == 0 and t % tm == 0 and t % GLA_CHUNK == 0
    gla_rows = min(GLA_ROWS, t)
    pos = jnp.arange(t, dtype=jnp.int32)
    cos_t, sin_t = _rope_tables(jnp.tile(pos, b))
    i = jnp.arange(tq, dtype=jnp.int32)
    rel0 = i[:, None] - i[None, :]
    bias = jnp.stack([_rel_bias_tile(p['rel_bias'], rel0), _rel_bias_tile(p['rel_bias'], rel0 + tq)])
    s0 = jnp.zeros((b, WV_B, WK_B), F32)
    ks, vs, cs, ss, gs = [], [], [], [], []
    y = None
    for l, lw in enumerate(lws):
        (qa, kaf, vaf, kab, vab, qg, kg, vg, gg, la, qm, cp, mla) = _inproj(x.reshape(n, d), lw, cos_t, sin_t, tm, BF)
        r3 = lambda a: a.reshape(b, t, a.shape[-1])
        oa = _diff_attn(r3(qa), r3(kab), r3(vab), bias, lw['gain'], lw['lam'], tq)
        oc = _mla_attn(qm.reshape(H_C, b, t, 256), r3(cp), lw['wuv'], tq)
        og, st = _gla(r3(qg), r3(kg), r3(vg), r3(la), r3(gg), s0, lw['gla_norm'], gla_rows, GLA_CHUNK, 16, BF)
        last = l == len(lws) - 1
        res = _ffn(x, oa, og, oc, lw, tm, final_w=p['final_norm'][None, :] if last else None)
        x, gt = res[0], res[1]
        if last:
            y = res[2]
        ks.append(kaf.reshape(b, t, H_A, 2 * DK_A))
        vs.append(vaf.reshape(b, t, H_A, DV_A))
        cs.append(mla.reshape(b, t, KV_LORA + ROPE_C))
        ss.append(_state_from_blockdiag(st))
        gs.append(gt[:, -1, 8 - (CONV_W - 1):, :])
    return y, [jnp.stack(a, axis=0) for a in (ks, vs, cs, ss, gs)]


def _state_to_blockdiag(s):
    b = s.shape[0]
    eye = jnp.eye(H_B, dtype=s.dtype)
    return jnp.einsum('bhdv,hg->bhvgd', s, eye).reshape(b, WV_B, WK_B)


def _state_from_blockdiag(st):
    b = st.shape[0]
    s5 = st.reshape(b, H_B, DV_B, H_B, DK_B)
    diag = jnp.stack([s5[:, hh, :, hh, :] for hh in range(H_B)], axis=1)
    return jnp.swapaxes(diag, -1, -2)


def _sample_trunk(x, p, lws, cache_k, cache_v, cache_c, state_gla, state_conv, page_table):
    b, t, d = x.shape
    n = b * t
    depth, n_pool, page = cache_k.shape[0], cache_k.shape[1], cache_k.shape[2]
    n_pages = page_table.shape[1]
    past_len = n_pages * page
    tm = min(ROW_TILE, n)
    g_pages = min(PAGES_PER_STEP, n_pages)
    assert n % tm == 0 and tm % t == 0 and n_pages % g_pages == 0 and t == 8 and page == MAX_DISTANCE
    pos = past_len + jnp.arange(t, dtype=jnp.int32)
    cos_t, sin_t = _rope_tables(jnp.tile(pos, b))
    kt_cache = jnp.transpose(cache_k, (0, 1, 3, 4, 2)).reshape(depth, n_pool, WA, page)
    vt_cache = jnp.transpose(cache_v, (0, 1, 3, 4, 2)).reshape(depth, n_pool, WV, page)
    ct_cache = jnp.transpose(cache_c, (0, 1, 3, 2))
    page_flat = page_table.reshape(-1).astype(jnp.int32)
    tok = jnp.arange(t, dtype=jnp.int32)
    col = jnp.arange(page, dtype=jnp.int32)
    rel_last = page + tok[:, None] - col[None, :]
    rel_new = jnp.where(col[None, :] < t, tok[:, None] - col[None, :], -1)
    rows = lambda bias: jnp.broadcast_to(bias[:, None], (H_A, 2, t, page)).reshape(2 * H_A * t, page)
    bias_last = jnp.pad(rows(_rel_bias_tile(p['rel_bias'], rel_last)), ((0, 0), ((g_pages - 1) * page, 0)))
    bias_new = rows(_rel_bias_tile(p['rel_bias'], rel_new))
    ks, vs, cs, ss, gs = [], [], [], [], []
    y = None
    x = x.reshape(1, n, d)
    for l, lw in enumerate(lws):
        (qa, kaf, vaf, _, _, qg, kg, vg, gg, la, qm, cp, mla) = _inproj(x.reshape(n, d), lw, cos_t, sin_t, tm, F32)
        r3 = lambda a: a.reshape(b, t, a.shape[-1])
        oa, oc = _dec_attn(l, page_flat, n_pages, r3(qa), qm.reshape(H_C, b, t, 256), r3(kaf), r3(vaf), r3(cp),
                           bias_last, bias_new, lw['gain'], lw['lam'], lw['wuv'],
                           kt_cache, vt_cache, ct_cache, g_pages)
        og, st = _gla(r3(qg), r3(kg), r3(vg), r3(la), r3(gg), _state_to_blockdiag(state_gla[l]),
                      lw['gla_norm'], t, t, t, F32)
        prev = jnp.repeat(jnp.swapaxes(state_conv[l], 0, 1), t, axis=1)[:, None]
        last = l == len(lws) - 1
        res = _ffn(x, oa.reshape(1, n, -1), og.reshape(1, n, -1), oc.reshape(1, n, -1), lw, tm, prev=prev,
                   final_w=p['final_norm'][None, :] if last else None)
        x, gt = res[0], res[1]
        if last:
            y = res[2]
        ks.append(kaf.reshape(b, t, H_A, 2 * DK_A))
        vs.append(vaf.reshape(b, t, H_A, DV_A))
        cs.append(mla.reshape(b, t, KV_LORA + ROPE_C))
        ss.append(_state_from_blockdiag(st))
        gs.append(gt.reshape(b, t, -1)[:, t - (CONV_W - 1):, :])
    return y.reshape(b, t, d), [jnp.stack(a, axis=0) for a in (ks, vs, cs, ss, gs)]


def kernel(x_prompt, x_sample, cache_diff_k, cache_diff_v, cache_mla, state_gla, state_ffn_conv, page_table,
           attn_norm, w_in, diff_lambda, diff_subln, rel_bias, w_gla_a2, b_gla_a, gla_norm, mla_q_norm, w_uq,
           mla_kv_norm, w_uk, w_uv, w_out, ffn_norm, w_gate, w_up, conv_w, conv_b, w_down, final_norm):
    p = {'attn_norm': attn_norm, 'w_in': w_in, 'diff_lambda': diff_lambda, 'diff_subln': diff_subln,
         'rel_bias': rel_bias, 'w_gla_a2': w_gla_a2, 'b_gla_a': b_gla_a, 'gla_norm': gla_norm,
         'mla_q_norm': mla_q_norm, 'w_uq': w_uq, 'mla_kv_norm': mla_kv_norm, 'w_uk': w_uk, 'w_uv': w_uv,
         'w_out': w_out, 'ffn_norm': ffn_norm, 'w_gate': w_gate, 'w_up': w_up, 'conv_w': conv_w,
         'conv_b': conv_b, 'w_down': w_down, 'final_norm': final_norm}
    lws = [_prep_layer(l, p) for l in range(w_in.shape[0])]
    y_p, new_p = _prompt_trunk(x_prompt, p, lws)
    y_s, new_s = _sample_trunk(x_sample, p, lws, cache_diff_k, cache_diff_v, cache_mla, state_gla,
                               state_ffn_conv, page_table)
    return (y_p, y_s, *new_p, *new_s)
```

```python
import functools
import math

import jax
import jax.numpy as jnp
import numpy as np
from jax import lax
from jax.experimental import pallas as pl
from jax.experimental.pallas import tpu as pltpu

BF = jnp.bfloat16
F32 = jnp.float32

H_A, DK_A, DV_A = 4, 32, 64
H_B, DK_B, DV_B = 4, 64, 128
GATE_RANK, GATE_TAU, GLA_CHUNK = 16, 16.0, 64
H_C, Q_LORA, KV_LORA, NOPE_C, ROPE_C, DV_C = 4, 192, 128, 64, 32, 64
ROPE_THETA = 10000.0
N_BUCKETS, MAX_DISTANCE = 32, 128
NORM_EPS = 1e-6
CONV_W = 3
LOG2E = math.log2(math.e)
ONES_ROWS = 16

WA = H_A * 2 * DK_A
WV = H_A * DV_A
WK_B = H_B * DK_B
WV_B = H_B * DV_B
LANE = 128
VMEM_LIMIT = 56 * 1024 * 1024
ROW_TILE = 256
ATTN_TILE = 512
GLA_ROWS = 512
PAGES_PER_STEP = 16

_C_QA, _C_KA, _C_VA, _C_QG, _C_KG, _C_VG, _C_GG = 0, 256, 512, 768, 1024, 1280, 1792
_C_AG, _C_CQ, _C_CKV, _C_KRA, _C_KRB, _C_END = 2304, 2432, 2688, 2816, 2944, 3072

_NT = (((1,), (1,)), ((), ()))
_TN = (((0,), (0,)), ((), ()))


def _cparams(sem):
    return pltpu.CompilerParams(dimension_semantics=sem, vmem_limit_bytes=VMEM_LIMIT)


def _const_spec(shape):
    nd = len(shape)
    return pl.BlockSpec(shape, lambda *_: (0,) * nd, pipeline_mode=pl.Buffered(1))


def _inproj_kernel(x_ref, nw_ref, w_ref, w2_ref, b2_ref, mqn_ref, wuq_ref, wuk_ref, mkn_ref,
                   cos_ref, sin_ref,
                   qa_ref, kaf_ref, vaf_ref, kab_ref, vab_ref, qg_ref, kg_ref, vg_ref, gg_ref,
                   la_ref, qm_ref, cp_ref, mla_ref):
    x = x_ref[...]
    h = x * lax.rsqrt(jnp.mean(x * x, axis=-1, keepdims=True) + NORM_EPS) * nw_ref[...]
    z = jnp.dot(h.astype(BF), w_ref[...], preferred_element_type=F32)

    qa_ref[...] = (z[:, _C_QA:_C_KA] * (DK_A ** -0.5 * LOG2E)).astype(qa_ref.dtype)
    ka = z[:, _C_KA:_C_VA]
    va = z[:, _C_VA:_C_QG]
    kaf_ref[...] = ka
    vaf_ref[...] = va
    kab_ref[...] = ka.astype(kab_ref.dtype)
    vab_ref[...] = va.astype(vab_ref.dtype)

    qg_ref[...] = z[:, _C_QG:_C_KG] * (DK_B ** -0.5)
    kg_ref[...] = z[:, _C_KG:_C_VG]
    vg_ref[...] = z[:, _C_VG:_C_GG].astype(vg_ref.dtype)
    gg_ref[...] = z[:, _C_GG:_C_AG]
    ag = z[:, _C_AG:_C_CQ]
    xg = jnp.dot(ag.astype(BF), w2_ref[...], preferred_element_type=F32) + b2_ref[...]
    la_ref[...] = (jnp.minimum(xg, 0.0) - jnp.log1p(jnp.exp(-jnp.abs(xg)))) * (1.0 / GATE_TAU)

    cos = cos_ref[...]
    sin = sin_ref[...]
    cq = z[:, _C_CQ:_C_CKV]
    cqn = cq * lax.rsqrt(jnp.sum(cq * cq, axis=-1, keepdims=True) * (1.0 / Q_LORA) + NORM_EPS) * mqn_ref[...]
    qc = jnp.dot(cqn.astype(BF), wuq_ref[...], preferred_element_type=F32)
    scale_c = (NOPE_C + ROPE_C) ** -0.5 * LOG2E
    q_rope = (qc[:, 256:384] * cos + qc[:, 384:512] * sin) * scale_c
    q_lat = jnp.dot(qc[:, 0:256].astype(BF), wuk_ref[...], preferred_element_type=F32) * scale_c
    lane = lax.broadcasted_iota(jnp.int32, q_rope.shape, 1)
    for hh in range(H_C):
        qm_ref[hh, :, 0:KV_LORA] = q_lat[:, hh * KV_LORA:(hh + 1) * KV_LORA].astype(qm_ref.dtype)
        qm_ref[hh, :, KV_LORA:2 * KV_LORA] = jnp.where(lane // ROPE_C == hh, q_rope, 0.0).astype(qm_ref.dtype)
    ckv = z[:, _C_CKV:_C_KRA]
    ckvn = ckv * lax.rsqrt(jnp.mean(ckv * ckv, axis=-1, keepdims=True) + NORM_EPS) * mkn_ref[...]
    krr = z[:, _C_KRA:_C_KRB] * cos + z[:, _C_KRB:_C_END] * sin
    cp_ref[:, 0:KV_LORA] = ckvn.astype(cp_ref.dtype)
    cp_ref[:, KV_LORA:2 * KV_LORA] = krr.astype(cp_ref.dtype)
    mla_ref[:, 0:KV_LORA] = ckvn
    mla_ref[:, KV_LORA:KV_LORA + ROPE_C] = krr[:, 0:ROPE_C]


def _inproj(x, lw, cos_t, sin_t, tm, act_dtype):
    n, d = x.shape
    grid = (n // tm,)
    row = lambda w: pl.BlockSpec((tm, w), lambda i: (i, 0))
    in_specs = [
        row(d), _const_spec((1, d)), _const_spec(lw['w_in'].shape), _const_spec(lw['w2'].shape),
        _const_spec((1, WK_B)), _const_spec((1, 256)), _const_spec(lw['wuq'].shape),
        _const_spec(lw['wuk'].shape), _const_spec((1, KV_LORA)), row(LANE), row(LANE),
    ]
    outs = [
        (WA, act_dtype), (WA, F32), (WV, F32), (WA, act_dtype), (WV, act_dtype),
        (WK_B, F32), (WK_B, F32), (WV_B, act_dtype), (WV_B, F32), (WK_B, F32),
    ]
    out_shape = [jax.ShapeDtypeStruct((n, w), dt) for w, dt in outs]
    out_specs = [row(w) for w, _ in outs]
    out_shape += [jax.ShapeDtypeStruct((H_C, n, 256), act_dtype), jax.ShapeDtypeStruct((n, 256), act_dtype),
                  jax.ShapeDtypeStruct((n, KV_LORA + ROPE_C), F32)]
    out_specs += [pl.BlockSpec((H_C, tm, 256), lambda i: (0, i, 0)), row(256),
                  pl.BlockSpec((tm, KV_LORA + ROPE_C), lambda i: (i, 0))]
    return pl.pallas_call(
        _inproj_kernel, grid=grid, in_specs=in_specs, out_specs=out_specs, out_shape=out_shape,
        compiler_params=_cparams(("parallel",)), name="inproj",
    )(x, lw['attn_norm'], lw['w_in'], lw['w2'], lw['b2'], lw['mqn'], lw['wuq'], lw['wuk'], lw['mkn'],
      cos_t, sin_t)


def _diff_attn_kernel(qi_ref, ki_ref, qT_ref, k_ref, vT_ref, bias_ref, gain_ref, lam_ref, o_ref,
                      qs_ref, m_ref, acc_ref, *, tq):
    p = pl.program_id(1)
    qi = qi_ref[p]
    ki = ki_ref[p]
    nmap = 2 * H_A
    va = DV_A + ONES_ROWS

    @pl.when(ki == 0)
    def _init():
        qT = qT_ref[...]
        row = lax.broadcasted_iota(jnp.int32, qT.shape, 0)
        zero = jnp.zeros_like(qT)
        for hm in range(nmap):
            qs_ref[:, hm * tq:(hm + 1) * tq] = jnp.where(row // DK_A == hm, qT, zero)
        m_ref[...] = jnp.full(m_ref.shape, -jnp.inf, F32)
        acc_ref[...] = jnp.zeros(acc_ref.shape, F32)

    def step(with_bias):
        k = k_ref[...]

        def qk(hm):
            s = jnp.dot(k, qs_ref[:, hm * tq:(hm + 1) * tq], preferred_element_type=F32)
            return s + bias_ref[hm // 2] if with_bias else s

        s_next = qk(0)
        for hm in range(nmap):
            hh = hm // 2
            s = s_next
            if hm + 1 < nmap:
                s_next = qk(hm + 1)
            m_old = m_ref[hm:hm + 1, :]
            m_new = jnp.maximum(m_old, jnp.max(s, axis=0, keepdims=True))
            alpha = jnp.exp2(m_old - m_new)
            pr = jnp.exp2(s - m_new).astype(BF)
            pv = jnp.dot(vT_ref[hh * va:(hh + 1) * va, :], pr, preferred_element_type=F32)
            acc_ref[hm] = alpha * acc_ref[hm] + pv
            m_ref[hm:hm + 1, :] = m_new

    pl.when(qi - ki <= 1)(lambda: step(True))
    pl.when(qi - ki > 1)(lambda: step(False))

    @pl.when(ki == qi)
    def _fin():
        lam = lam_ref[...]
        outs = []
        for hh in range(H_A):
            a1 = acc_ref[2 * hh]
            a2 = acc_ref[2 * hh + 1]
            dh = a1[0:DV_A] / a1[DV_A:DV_A + 1] - lam * (a2[0:DV_A] / a2[DV_A:DV_A + 1])
            ms = jnp.mean(dh * dh, axis=0, keepdims=True)
            outs.append(dh * lax.rsqrt(ms + NORM_EPS))
        out_t = jnp.concatenate(outs, axis=0)
        o_ref[...] = (out_t.T * gain_ref[...]).astype(o_ref.dtype)


def _causal_pairs(nq):
    qi = np.concatenate([np.full(i + 1, i, np.int32) for i in range(nq)])
    ki = np.concatenate([np.arange(i + 1, dtype=np.int32) for i in range(nq)])
    return jnp.asarray(qi), jnp.asarray(ki)


def _diff_attn(q_t, k, v_t, bias_t, gain, lam, tq):
    b, _, t = q_t.shape
    nq = t // tq
    va = DV_A + ONES_ROWS
    qi, ki = _causal_pairs(nq)
    grid_spec = pltpu.PrefetchScalarGridSpec(
        num_scalar_prefetch=2, grid=(b, int(qi.shape[0])),
        in_specs=[
            pl.BlockSpec((None, WA, tq), lambda bb, p, qi, ki: (bb, 0, qi[p])),
            pl.BlockSpec((None, tq, WA), lambda bb, p, qi, ki: (bb, ki[p], 0)),
            pl.BlockSpec((None, H_A * va, tq), lambda bb, p, qi, ki: (bb, 0, ki[p])),
            pl.BlockSpec((None, H_A, tq, tq), lambda bb, p, qi, ki: (jnp.minimum(qi[p] - ki[p], 1), 0, 0, 0)),
            pl.BlockSpec((1, WV), lambda bb, p, qi, ki: (0, 0)),
            pl.BlockSpec((1, 1), lambda bb, p, qi, ki: (0, 0)),
        ],
        out_specs=pl.BlockSpec((None, tq, WV), lambda bb, p, qi, ki: (bb, qi[p], 0)),
        scratch_shapes=[
            pltpu.VMEM((WA, 2 * H_A * tq), BF), pltpu.VMEM((2 * H_A, tq), F32),
            pltpu.VMEM((2 * H_A, va, tq), F32),
        ],
    )
    return pl.pallas_call(
        functools.partial(_diff_attn_kernel, tq=tq), grid_spec=grid_spec,
        out_shape=jax.ShapeDtypeStruct((b, t, WV), BF),
        compiler_params=_cparams(("parallel", "arbitrary")), name="diff_attn",
    )(qi, ki, q_t, k, v_t, bias_t, gain, lam)


def _mla_attn_kernel(qi_ref, ki_ref, qmT_ref, c_ref, cT_ref, wuvT_ref, o_ref, m_ref, acc_ref, *, tq):
    p = pl.program_id(1)
    qi = qi_ref[p]
    ki = ki_ref[p]

    @pl.when(ki == 0)
    def _init():
        m_ref[...] = jnp.full(m_ref.shape, -jnp.inf, F32)
        acc_ref[...] = jnp.zeros(acc_ref.shape, F32)

    def step(masked):
        c = c_ref[...]
        if masked:
            causal = (lax.broadcasted_iota(jnp.int32, (tq, tq), 0) <= lax.broadcasted_iota(jnp.int32, (tq, tq), 1))

        def qk(hh):
            s = jnp.dot(c, qmT_ref[hh], preferred_element_type=F32)
            return jnp.where(causal, s, -jnp.inf) if masked else s

        s_next = qk(0)
        for hh in range(H_C):
            s = s_next
            if hh + 1 < H_C:
                s_next = qk(hh + 1)
            m_old = m_ref[hh:hh + 1, :]
            m_new = jnp.maximum(m_old, jnp.max(s, axis=0, keepdims=True))
            alpha = jnp.exp2(m_old - m_new)
            pr = jnp.exp2(s - m_new).astype(BF)
            acc_ref[hh] = alpha * acc_ref[hh] + jnp.dot(cT_ref[...], pr, preferred_element_type=F32)
            m_ref[hh:hh + 1, :] = m_new

    pl.when(ki == qi)(lambda: step(True))
    pl.when(ki != qi)(lambda: step(False))

    @pl.when(ki == qi)
    def _fin():
        out_t = jnp.zeros((H_C * DV_C, tq), F32)
        for hh in range(H_C):
            a = acc_ref[hh]
            o_lat = (a[0:KV_LORA] / a[KV_LORA:KV_LORA + 1]).astype(BF)
            out_t = out_t + jnp.dot(wuvT_ref[hh], o_lat, preferred_element_type=F32)
        o_ref[...] = out_t.T.astype(o_ref.dtype)


def _mla_attn(qm_t, c, c_t, wuv_t, tq):
    b, t, _ = c.shape
    nq = t // tq
    qi, ki = _causal_pairs(nq)
    rows = KV_LORA + ONES_ROWS
    grid_spec = pltpu.PrefetchScalarGridSpec(
        num_scalar_prefetch=2, grid=(b, int(qi.shape[0])),
        in_specs=[
            pl.BlockSpec((H_C, None, 256, tq), lambda bb, p, qi, ki: (0, bb, 0, qi[p])),
            pl.BlockSpec((None, tq, 256), lambda bb, p, qi, ki: (bb, ki[p], 0)),
            pl.BlockSpec((None, rows, tq), lambda bb, p, qi, ki: (bb, 0, ki[p])),
            pl.BlockSpec((H_C, H_C * DV_C, KV_LORA), lambda bb, p, qi, ki: (0, 0, 0)),
        ],
        out_specs=pl.BlockSpec((None, tq, H_C * DV_C), lambda bb, p, qi, ki: (bb, qi[p], 0)),
        scratch_shapes=[pltpu.VMEM((H_C, tq), F32), pltpu.VMEM((H_C, rows, tq), F32)],
    )
    return pl.pallas_call(
        functools.partial(_mla_attn_kernel, tq=tq), grid_spec=grid_spec,
        out_shape=jax.ShapeDtypeStruct((b, t, H_C * DV_C), BF),
        compiler_params=_cparams(("parallel", "arbitrary")), name="mla_attn",
    )(qi, ki, qm_t, c, c_t, wuv_t)


def _gla_kernel(q_ref, k_ref, v_ref, g_ref, gate_ref, s0_ref, nw_ref, o_ref, so_ref, st_ref,
                *, chunk, sub, nchunk):
    t = pl.program_id(1)

    @pl.when(t == 0)
    def _init():
        st_ref[...] = s0_ref[...]

    C, c = chunk, sub
    tri = (lax.broadcasted_iota(jnp.int32, (C, C), 0) >= lax.broadcasted_iota(jnp.int32, (C, C), 1)).astype(F32)
    ind = (lax.broadcasted_iota(jnp.int32, (WK_B, WV_B), 0) // DK_B
           == lax.broadcasted_iota(jnp.int32, (WK_B, WV_B), 1) // DV_B).astype(BF)
    bdmask = (lax.broadcasted_iota(jnp.int32, (WV_B, WK_B), 0) // DV_B
              == lax.broadcasted_iota(jnp.int32, (WV_B, WK_B), 1) // DK_B)
    lane_k = lax.broadcasted_iota(jnp.int32, (c, WK_B), 1)
    row_c = lax.broadcasted_iota(jnp.int32, (c, WK_B), 0)
    nw = nw_ref[...]

    def do_chunk(n, carry):
        r = pl.multiple_of(n * C, C)
        q = q_ref[pl.ds(r, C), :]
        k = k_ref[pl.ds(r, C), :]
        v = v_ref[pl.ds(r, C), :].astype(BF)
        g = g_ref[pl.ds(r, C), :]
        b = jnp.dot(tri, g, preferred_element_type=F32, precision=lax.Precision.HIGHEST)
        b_last = b[C - 1:C, :]
        st = st_ref[...]
        qt = (q * jnp.exp(b)).astype(BF)
        o = lax.dot_general(qt, st.astype(BF), _NT, preferred_element_type=F32)
        kt = (k * jnp.exp(b_last - b)).astype(BF)
        u = lax.dot_general(v, kt, _TN, preferred_element_type=F32)
        st_ref[...] = st * jnp.exp(b_last) + jnp.where(bdmask, u, 0.0)
        rows = []
        for blk in range(C // c):
            lo, hi = blk * c, (blk + 1) * c
            b_i, q_i, k_i = b[lo:hi], q[lo:hi], k[lo:hi]
            v_i = v[lo:hi].astype(F32)
            es = []
            for j in range(c):
                e = jnp.exp(jnp.minimum(b_i - b_i[j:j + 1], 0.0)) * q_i * k_i[j:j + 1]
                es.append(jnp.where(row_c >= j, e, 0.0))
            rr = jnp.dot(jnp.concatenate(es, axis=0).astype(BF), ind, preferred_element_type=F32)
            od = rr[0:c] * v_i[0:1]
            for j in range(1, c):
                od = od + rr[j * c:(j + 1) * c] * v_i[j:j + 1]
            if blk > 0:
                b_s = b[lo - 1:lo]
                qh = q_i * jnp.exp(b_i - b_s)
                kh = (k[0:lo] * jnp.exp(b_s - b[0:lo])).astype(BF)
                qs = jnp.concatenate([jnp.where(lane_k // DK_B == hh, qh, 0.0) for hh in range(H_B)], axis=0)
                a = lax.dot_general(qs.astype(BF), kh, _NT, preferred_element_type=F32)
                pv = jnp.dot(a.astype(BF), v[0:lo], preferred_element_type=F32)
                od = od + jnp.concatenate(
                    [pv[hh * c:(hh + 1) * c, hh * DV_B:(hh + 1) * DV_B] for hh in range(H_B)], axis=1)
            rows.append(od)
        o = o + (jnp.concatenate(rows, axis=0) if len(rows) > 1 else rows[0])
        gate = gate_ref[pl.ds(r, C), :]
        outs = []
        for hh in range(H_B):
            oh = o[:, hh * DV_B:(hh + 1) * DV_B]
            outs.append(oh * lax.rsqrt(jnp.mean(oh * oh, axis=-1, keepdims=True) + NORM_EPS) * nw)
        y = jnp.concatenate(outs, axis=1) * (gate * jax.nn.sigmoid(gate))
        o_ref[pl.ds(r, C), :] = y.astype(o_ref.dtype)
        return carry

    if nchunk == 1:
        do_chunk(0, 0)
    else:
        lax.fori_loop(0, nchunk, do_chunk, 0)

    @pl.when(t == pl.num_programs(1) - 1)
    def _fin():
        so_ref[...] = st_ref[...]


def _gla(q, k, v, g, gate, s0t, nw, rows, chunk, sub, out_dtype):
    b, t, _ = q.shape
    nt = t // rows
    blk = lambda w: pl.BlockSpec((None, rows, w), lambda bb, tt: (bb, tt, 0))
    st_spec = pl.BlockSpec((None, WV_B, WK_B), lambda bb, tt: (bb, 0, 0))
    return pl.pallas_call(
        functools.partial(_gla_kernel, chunk=chunk, sub=sub, nchunk=rows // chunk),
        grid=(b, nt),
        in_specs=[blk(WK_B), blk(WK_B), blk(WV_B), blk(WK_B), blk(WV_B), st_spec,
                  pl.BlockSpec((1, DV_B), lambda bb, tt: (0, 0))],
        out_specs=[blk(WV_B), st_spec],
        out_shape=[jax.ShapeDtypeStruct((b, t, WV_B), out_dtype), jax.ShapeDtypeStruct((b, WV_B, WK_B), F32)],
        scratch_shapes=[pltpu.VMEM((WV_B, WK_B), F32)],
        compiler_params=_cparams(("parallel", "arbitrary")), name="gla",
    )(q, k, v, g, gate, s0t, nw)


def _ffn_kernel(*refs, tm, per_seq_rows, with_final):
    it = iter(refs)
    x_ref, oa_ref, og_ref, oc_ref, wo_ref, fnw_ref, wg_ref, wu_ref, cw_ref, cb_ref, wd_ref = (next(it) for _ in range(11))
    prev_ref = next(it) if per_seq_rows else None
    finw_ref = next(it) if with_final else None
    xo_ref, gt_ref = next(it), next(it)
    y_ref = next(it) if with_final else None
    carry_ref = None if per_seq_rows else next(it)

    x = x_ref[...]
    mix = (jnp.dot(oa_ref[...].astype(BF), wo_ref[0:WV, :], preferred_element_type=F32)
           + jnp.dot(og_ref[...].astype(BF), wo_ref[WV:WV + WV_B, :], preferred_element_type=F32)
           + jnp.dot(oc_ref[...].astype(BF), wo_ref[WV + WV_B:, :], preferred_element_type=F32))
    xm = x + mix
    hb = (xm * lax.rsqrt(jnp.mean(xm * xm, axis=-1, keepdims=True) + NORM_EPS) * fnw_ref[...]).astype(BF)
    g = jnp.dot(hb, wg_ref[...], preferred_element_type=F32)
    u = jnp.dot(hb, wu_ref[...], preferred_element_type=F32)
    row = lax.broadcasted_iota(jnp.int32, g.shape, 0)
    g1 = pltpu.roll(g, 1, axis=0)
    g2 = pltpu.roll(g, 2, axis=0)
    if per_seq_rows:
        rm = row % per_seq_rows
        p0, p1 = prev_ref[0], prev_ref[1]
        g1 = jnp.where(rm == 0, p1, g1)
        g2 = jnp.where(rm == 0, p0, jnp.where(rm == 1, p1, g2))
        gt_ref[...] = g
    else:
        @pl.when(pl.program_id(1) == 0)
        def _zero():
            carry_ref[...] = jnp.zeros(carry_ref.shape, F32)
        p0, p1 = carry_ref[6:7, :], carry_ref[7:8, :]
        g1 = jnp.where(row == 0, p1, g1)
        g2 = jnp.where(row == 0, p0, jnp.where(row == 1, p1, g2))
        tail = g[tm - 8:tm, :]
        carry_ref[...] = tail
        gt_ref[...] = tail
    gc = cb_ref[...] + g2 * cw_ref[0:1, :] + g1 * cw_ref[1:2, :] + g * cw_ref[2:3, :]
    act = (gc * jax.nn.sigmoid(gc) * u).astype(BF)
    xo = xm + jnp.dot(act, wd_ref[...], preferred_element_type=F32)
    xo_ref[...] = xo
    if with_final:
        y_ref[...] = xo * lax.rsqrt(jnp.mean(xo * xo, axis=-1, keepdims=True) + NORM_EPS) * finw_ref[...]


def _ffn(x, oa, og, oc, lw, tm, prev=None, final_w=None):
    b, t, d = x.shape
    f = lw['w_gate'].shape[1]
    nt = t // tm
    per_seq = prev is not None
    blk = lambda w: pl.BlockSpec((None, tm, w), lambda bb, tt: (bb, tt, 0))
    in_specs = [blk(d), blk(WV), blk(WV_B), blk(H_C * DV_C), _const_spec(lw['w_out'].shape), _const_spec((1, d)),
                _const_spec(lw['w_gate'].shape), _const_spec(lw['w_up'].shape), _const_spec((CONV_W, f)),
                _const_spec((1, f)), _const_spec(lw['w_down'].shape)]
    args = [x, oa, og, oc, lw['w_out'], lw['ffn_norm'], lw['w_gate'], lw['w_up'], lw['conv_w'], lw['conv_b'],
            lw['w_down']]
    if per_seq:
        in_specs.append(pl.BlockSpec((2, None, tm, f), lambda bb, tt: (0, bb, tt, 0)))
        args.append(prev)
    if final_w is not None:
        in_specs.append(_const_spec((1, d)))
        args.append(final_w)
    gr = tm if per_seq else 8
    out_shape = [jax.ShapeDtypeStruct((b, t, d), F32), jax.ShapeDtypeStruct((b, nt, gr, f), F32)]
    out_specs = [blk(d), pl.BlockSpec((None, None, gr, f), lambda bb, tt: (bb, tt, 0, 0))]
    if final_w is not None:
        out_shape.append(jax.ShapeDtypeStruct((b, t, d), F32))
        out_specs.append(blk(d))
    scratch = [] if per_seq else [pltpu.VMEM((8, f), F32)]
    return pl.pallas_call(
        functools.partial(_ffn_kernel, tm=tm, per_seq_rows=8 if per_seq else 0, with_final=final_w is not None),
        grid=(b, nt), in_specs=in_specs, out_specs=out_specs, out_shape=out_shape, scratch_shapes=scratch,
        compiler_params=_cparams(("parallel", "arbitrary")), name="ffn",
    )(*args)


def _dec_attn_kernel(*refs, g_pages, t_new):
    pt_ref = refs[0]
    (qa_ref, qm_ref, kn_ref, vn_ref, cn_ref, bl_ref, bn_ref, gain_ref, lam_ref, wuv_ref) = refs[1:11]
    k_refs = refs[11:11 + g_pages]
    v_refs = refs[11 + g_pages:11 + 2 * g_pages]
    c_refs = refs[11 + 2 * g_pages:11 + 3 * g_pages]
    oa_ref, oc_ref = refs[11 + 3 * g_pages:13 + 3 * g_pages]
    (qs_ref, qms_ref, kt_ref, vt_ref, ct_ref, ma_ref, la_ref, acca_ref, mc_ref, lc_ref, accc_ref) = refs[13 + 3 * g_pages:]
    del pt_ref
    j = pl.program_id(1)
    nj = pl.num_programs(1)
    ra = 2 * H_A * t_new
    rc = H_C * t_new
    page = k_refs[0].shape[-1]
    wc = KV_LORA + ROPE_C

    @pl.when(j == 0)
    def _init():
        q = qa_ref[...]
        lane = lax.broadcasted_iota(jnp.int32, q.shape, 1)
        qs = jnp.concatenate([jnp.where(lane // DK_A == hm, q, 0.0) for hm in range(2 * H_A)], axis=0)
        qs_ref[...] = qs.astype(BF)
        qms_ref[...] = qm_ref[...].reshape(rc, wc).astype(BF)
        ma_ref[...] = jnp.full(ma_ref.shape, -jnp.inf, F32)
        la_ref[...] = jnp.zeros(la_ref.shape, F32)
        acca_ref[...] = jnp.zeros(acca_ref.shape, F32)
        mc_ref[...] = jnp.full(mc_ref.shape, -jnp.inf, F32)
        lc_ref[...] = jnp.zeros(lc_ref.shape, F32)
        accc_ref[...] = jnp.zeros(accc_ref.shape, F32)

    def online(s, m_ref, l_ref, acc_ref, pv):
        m_old = m_ref[...]
        m_new = jnp.maximum(m_old, jnp.max(s, axis=-1, keepdims=True))
        alpha = jnp.exp2(m_old - m_new)
        pr = jnp.exp2(s - m_new)
        l_ref[...] = alpha * l_ref[...] + jnp.sum(pr, axis=-1, keepdims=True)
        acc_ref[...] = alpha * acc_ref[...] + pv(pr.astype(BF))
        m_ref[...] = m_new

    def stage(dst_ref, src_refs):
        for gg in range(g_pages):
            dst_ref[:, gg * page:(gg + 1) * page] = src_refs[gg][...].astype(BF)

    def past(with_bias):
        stage(kt_ref, k_refs)
        s = jnp.dot(qs_ref[...], kt_ref[...], preferred_element_type=F32)
        if with_bias:
            s = s + bl_ref[...]
        stage(ct_ref, c_refs)
        sc = jnp.dot(qms_ref[...], ct_ref[...], preferred_element_type=F32)
        stage(vt_ref, v_refs)
        online(s, ma_ref, la_ref, acca_ref,
               lambda pr: lax.dot_general(pr, vt_ref[...], _NT, preferred_element_type=F32))
        online(sc, mc_ref, lc_ref, accc_ref,
               lambda pr: lax.dot_general(pr, ct_ref[0:KV_LORA, :], _NT, preferred_element_type=F32))

    pl.when(j == nj - 1)(lambda: past(True))
    pl.when(j != nj - 1)(lambda: past(False))

    @pl.when(j == nj - 1)
    def _fin():
        pad = page - t_new
        kn = jnp.concatenate([kn_ref[...], jnp.zeros((pad, WA), F32)], axis=0).astype(BF)
        vn = jnp.concatenate([vn_ref[...], jnp.zeros((pad, WV), F32)], axis=0).astype(BF)
        cn = jnp.concatenate([cn_ref[...], jnp.zeros((pad, wc), F32)], axis=0).astype(BF)
        s = lax.dot_general(qs_ref[...], kn, _NT, preferred_element_type=F32) + bn_ref[...]
        online(s, ma_ref, la_ref, acca_ref, lambda pr: jnp.dot(pr, vn, preferred_element_type=F32))
        tok = lax.broadcasted_iota(jnp.int32, (rc, page), 0) % t_new
        col = lax.broadcasted_iota(jnp.int32, (rc, page), 1)
        sc = lax.dot_general(qms_ref[...], cn, _NT, preferred_element_type=F32)
        sc = jnp.where(col <= tok, sc, -jnp.inf)
        online(sc, mc_ref, lc_ref, accc_ref, lambda pr: jnp.dot(pr, cn[:, 0:KV_LORA], preferred_element_type=F32))

        lam = lam_ref[...]
        lane = lax.broadcasted_iota(jnp.int32, (t_new, WV), 1)
        o = acca_ref[...] / la_ref[...]
        out = jnp.zeros((t_new, WV), F32)
        for hh in range(H_A):
            a0 = 2 * hh * t_new
            dh = o[a0:a0 + t_new] - lam * o[a0 + t_new:a0 + 2 * t_new]
            hmask = lane // DV_A == hh
            ms = jnp.sum(jnp.where(hmask, dh * dh, 0.0), axis=-1, keepdims=True) * (1.0 / DV_A)
            out = jnp.where(hmask, dh * lax.rsqrt(ms + NORM_EPS), out)
        oa_ref[...] = out * gain_ref[...]
        ol = (accc_ref[...] / lc_ref[...]).astype(BF)
        oc = jnp.zeros((t_new, H_C * DV_C), F32)
        for hh in range(H_C):
            oc = oc + jnp.dot(ol[hh * t_new:(hh + 1) * t_new], wuv_ref[hh], preferred_element_type=F32)
        oc_ref[...] = oc


def _dec_attn(layer, page_flat, n_pages, qa, qm, kn, vn, cn, bias_last, bias_new, gain, lam, wuv,
              kt_cache, vt_cache, ct_cache, g_pages):
    b, t_new, _ = qa.shape
    page = kt_cache.shape[-1]
    ng = n_pages // g_pages
    ra, rc = 2 * H_A * t_new, H_C * t_new
    seq = lambda w: pl.BlockSpec((None, t_new, w), lambda bb, jj, pt: (bb, 0, 0))
    const = lambda shape: pl.BlockSpec(shape, lambda bb, jj, pt: (0,) * len(shape))

    def page_spec(rows, gg):
        return pl.BlockSpec((None, None, rows, page),
                            lambda bb, jj, pt: (layer, pt[bb * n_pages + jj * g_pages + gg], 0, 0))

    wc = KV_LORA + ROPE_C
    in_specs = [seq(WA), pl.BlockSpec((H_C, None, t_new, wc), lambda bb, jj, pt: (0, bb, 0, 0)),
                seq(WA), seq(WV), seq(wc), const((ra, g_pages * page)), const((ra, page)),
                const((1, WV)), const((1, 1)), const((H_C, KV_LORA, H_C * DV_C))]
    in_specs += [page_spec(WA, gg) for gg in range(g_pages)]
    in_specs += [page_spec(WV, gg) for gg in range(g_pages)]
    in_specs += [page_spec(KV_LORA + ROPE_C, gg) for gg in range(g_pages)]
    grid_spec = pltpu.PrefetchScalarGridSpec(
        num_scalar_prefetch=1, grid=(b, ng), in_specs=in_specs,
        out_specs=[seq(WV), seq(H_C * DV_C)],
        scratch_shapes=[
            pltpu.VMEM((ra, WA), BF), pltpu.VMEM((rc, wc), BF),
            pltpu.VMEM((WA, g_pages * page), BF), pltpu.VMEM((WV, g_pages * page), BF),
            pltpu.VMEM((wc, g_pages * page), BF),
            pltpu.VMEM((ra, 1), F32), pltpu.VMEM((ra, 1), F32), pltpu.VMEM((ra, WV), F32),
            pltpu.VMEM((rc, 1), F32), pltpu.VMEM((rc, 1), F32), pltpu.VMEM((rc, KV_LORA), F32),
        ],
    )
    return pl.pallas_call(
        functools.partial(_dec_attn_kernel, g_pages=g_pages, t_new=t_new), grid_spec=grid_spec,
        out_shape=[jax.ShapeDtypeStruct((b, t_new, WV), F32), jax.ShapeDtypeStruct((b, t_new, H_C * DV_C), F32)],
        compiler_params=_cparams(("parallel", "arbitrary")), name="dec_attn",
    )(page_flat, qa, qm, kn, vn, cn, bias_last, bias_new, gain, lam, wuv,
      *([kt_cache] * g_pages), *([vt_cache] * g_pages), *([ct_cache] * g_pages))


def _split_points():
    sizes = [WA, WA, WV, WK_B, WK_B, WV_B, WV_B, GATE_RANK, Q_LORA, KV_LORA, ROPE_C]
    return [int(s) for s in np.cumsum(sizes)]


def _pad_cols(a, w):
    return jnp.pad(a, ((0, 0), (0, w - a.shape[1])))


def _prep_layer(l, p):
    sp = _split_points()
    w_in = p['w_in'][l]
    half = ROPE_C // 2
    swap = np.concatenate([np.arange(half, ROPE_C), np.arange(0, half)])
    kr = w_in[:, sp[9]:sp[10]]
    w_cat = jnp.concatenate([
        w_in[:, :sp[6]], _pad_cols(w_in[:, sp[6]:sp[7]], LANE), _pad_cols(w_in[:, sp[7]:sp[8]], 256),
        w_in[:, sp[8]:sp[9]], jnp.tile(kr, (1, H_C)), jnp.tile(kr[:, swap], (1, H_C))], axis=1)
    assert w_cat.shape[1] == _C_END
    w_uq = p['w_uq'][l].reshape(Q_LORA, H_C, NOPE_C + ROPE_C)
    rope_cols = w_uq[:, :, NOPE_C:]
    wuq = jnp.concatenate([w_uq[:, :, :NOPE_C].reshape(Q_LORA, -1), rope_cols.reshape(Q_LORA, -1),
                           rope_cols[:, :, swap].reshape(Q_LORA, -1)], axis=1)
    wuq = jnp.pad(wuq, ((0, 256 - Q_LORA), (0, 0)))
    eye = jnp.eye(H_C, dtype=F32)
    wuk = jnp.einsum('hcn,hg->hngc', p['w_uk'][l], eye).reshape(H_C * NOPE_C, H_C * KV_LORA)
    wuv = jnp.einsum('hcv,hg->hcgv', p['w_uv'][l], eye).reshape(H_C, KV_LORA, H_C * DV_C)
    lam_init = 0.8 - 0.6 * math.exp(-0.3 * l)
    lq1, lk1, lq2, lk2 = p['diff_lambda'][l].astype(F32)
    lam = jnp.exp(jnp.sum(lq1 * lk1)) - jnp.exp(jnp.sum(lq2 * lk2)) + lam_init
    return {
        'attn_norm': p['attn_norm'][l][None, :], 'w_in': w_cat.astype(BF),
        'w2': jnp.pad(p['w_gla_a2'][l], ((0, LANE - GATE_RANK), (0, 0))).astype(BF),
        'b2': p['b_gla_a'][l][None, :], 'mqn': _pad_cols(p['mla_q_norm'][l][None, :], 256),
        'wuq': wuq.astype(BF), 'wuk': wuk.astype(BF), 'mkn': p['mla_kv_norm'][l][None, :],
        'wuv': wuv.astype(BF), 'wuv_t': jnp.swapaxes(wuv, 1, 2).astype(BF), 'gain': jnp.tile(p['diff_subln'][l] * (1.0 - lam_init), H_A)[None, :],
        'lam': lam.reshape(1, 1), 'gla_norm': p['gla_norm'][l][None, :],
        'w_out': p['w_out'][l].astype(BF), 'ffn_norm': p['ffn_norm'][l][None, :],
        'w_gate': p['w_gate'][l].astype(BF), 'w_up': p['w_up'][l].astype(BF),
        'conv_w': p['conv_w'][l], 'conv_b': p['conv_b'][l][None, :], 'w_down': p['w_down'][l].astype(BF),
    }


def _rope_tables(pos):
    half = ROPE_C // 2
    inv = ROPE_THETA ** (-jnp.arange(half, dtype=F32) / half)
    ang = pos.astype(F32)[:, None] * inv
    cos, sin = jnp.cos(ang), jnp.sin(ang)
    cos_t = jnp.tile(jnp.concatenate([cos, cos], axis=1), (1, LANE // ROPE_C))
    sin_t = jnp.tile(jnp.concatenate([-sin, sin], axis=1), (1, LANE // ROPE_C))
    return cos_t, sin_t


def _t5_bucket(rel):
    n = jnp.maximum(rel, 0)
    max_exact = N_BUCKETS // 2
    nf = jnp.maximum(n, max_exact).astype(F32)
    large = max_exact + (jnp.log(nf / max_exact) / math.log(MAX_DISTANCE / max_exact)
                         * (N_BUCKETS - max_exact)).astype(jnp.int32)
    large = jnp.minimum(large, N_BUCKETS - 1)
    return jnp.where(n < max_exact, n, large)


def _rel_bias_tile(rel_table, rel):
    shifted = (rel_table - rel_table[N_BUCKETS - 1]) * LOG2E
    bias = jnp.moveaxis(shifted[_t5_bucket(rel)].astype(F32), -1, 0)
    return jnp.where(rel >= 0, bias, -jnp.inf)


def _toeplitz_bias_t(rel_table, offset, tq):
    ln = 2 * tq
    m = jnp.arange(ln, dtype=jnp.int32)
    m = jnp.where(m < tq, m, m - ln)
    w = _rel_bias_tile(rel_table, offset + m)
    sheared = jnp.tile(w, (1, tq))[:, :tq * (ln - 1)].reshape(w.shape[0], tq, ln - 1)
    return sheared[:, :, :tq]


def _prompt_trunk(x, p, lws):
    b, t, d = x.shape
    n = b * t
    tm = min(ROW_TILE, t)
    tq = min(ATTN_TILE, t)
    assert tq >= MAX_DISTANCE and t % tq == 0 and t % tm == 0 and t % GLA_CHUNK == 0
    gla_rows = min(GLA_ROWS, t)
    pos = jnp.arange(t, dtype=jnp.int32)
    cos_t, sin_t = _rope_tables(jnp.tile(pos, b))
    bias_t = jnp.stack([_toeplitz_bias_t(p['rel_bias'], 0, tq), _toeplitz_bias_t(p['rel_bias'], tq, tq)])
    s0 = jnp.zeros((b, WV_B, WK_B), F32)
    ones_a = jnp.ones((b, H_A, ONES_ROWS, t), BF)
    ones_c = jnp.ones((b, ONES_ROWS, t), BF)
    ks, vs, cs, ss, gs = [], [], [], [], []
    y = None
    for l, lw in enumerate(lws):
        (qa, kaf, vaf, kab, vab, qg, kg, vg, gg, la, qm, cp, mla) = _inproj(x.reshape(n, d), lw, cos_t, sin_t, tm, BF)
        r3 = lambda a: a.reshape(b, t, a.shape[-1])
        q_t = jnp.swapaxes(r3(qa), 1, 2)
        v_t = jnp.concatenate([jnp.swapaxes(r3(vab), 1, 2).reshape(b, H_A, DV_A, t), ones_a], axis=2)
        oa = _diff_attn(q_t, r3(kab), v_t.reshape(b, H_A * (DV_A + ONES_ROWS), t), bias_t, lw['gain'], lw['lam'], tq)
        c3 = r3(cp)
        c_t = jnp.concatenate([jnp.swapaxes(c3[:, :, 0:KV_LORA], 1, 2), ones_c], axis=1)
        oc = _mla_attn(jnp.swapaxes(qm.reshape(H_C, b, t, 256), 2, 3), c3, c_t, lw['wuv_t'], tq)
        og, st = _gla(r3(qg), r3(kg), r3(vg), r3(la), r3(gg), s0, lw['gla_norm'], gla_rows, GLA_CHUNK, 16, BF)
        last = l == len(lws) - 1
        res = _ffn(x, oa, og, oc, lw, tm, final_w=p['final_norm'][None, :] if last else None)
        x, gt = res[0], res[1]
        if last:
            y = res[2]
        ks.append(kaf.reshape(b, t, H_A, 2 * DK_A))
        vs.append(vaf.reshape(b, t, H_A, DV_A))
        cs.append(mla.reshape(b, t, KV_LORA + ROPE_C))
        ss.append(_state_from_blockdiag(st))
        gs.append(gt[:, -1, 8 - (CONV_W - 1):, :])
    return y, [jnp.stack(a, axis=0) for a in (ks, vs, cs, ss, gs)]


def _state_to_blockdiag(s):
    b = s.shape[0]
    eye = jnp.eye(H_B, dtype=s.dtype)
    return jnp.einsum('bhdv,hg->bhvgd', s, eye).reshape(b, WV_B, WK_B)


def _state_from_blockdiag(st):
    b = st.shape[0]
    s5 = st.reshape(b, H_B, DV_B, H_B, DK_B)
    diag = jnp.stack([s5[:, hh, :, hh, :] for hh in range(H_B)], axis=1)
    return jnp.swapaxes(diag, -1, -2)


def _sample_trunk(x, p, lws, cache_k, cache_v, cache_c, state_gla, state_conv, page_table):
    b, t, d = x.shape
    n = b * t
    depth, n_pool, page = cache_k.shape[0], cache_k.shape[1], cache_k.shape[2]
    n_pages = page_table.shape[1]
    past_len = n_pages * page
    tm = min(ROW_TILE, n)
    g_pages = min(PAGES_PER_STEP, n_pages)
    assert n % tm == 0 and tm % t == 0 and n_pages % g_pages == 0 and t == 8 and page == MAX_DISTANCE
    pos = past_len + jnp.arange(t, dtype=jnp.int32)
    cos_t, sin_t = _rope_tables(jnp.tile(pos, b))
    kt_cache = jnp.transpose(cache_k, (0, 1, 3, 4, 2)).reshape(depth, n_pool, WA, page)
    vt_cache = jnp.transpose(cache_v, (0, 1, 3, 4, 2)).reshape(depth, n_pool, WV, page)
    ct_cache = jnp.transpose(cache_c, (0, 1, 3, 2))
    page_flat = page_table.reshape(-1).astype(jnp.int32)
    tok = jnp.arange(t, dtype=jnp.int32)
    col = jnp.arange(page, dtype=jnp.int32)
    rel_last = page + tok[:, None] - col[None, :]
    rel_new = jnp.where(col[None, :] < t, tok[:, None] - col[None, :], -1)
    rows = lambda bias: jnp.broadcast_to(bias[:, None], (H_A, 2, t, page)).reshape(2 * H_A * t, page)
    bias_last = jnp.pad(rows(_rel_bias_tile(p['rel_bias'], rel_last)), ((0, 0), ((g_pages - 1) * page, 0)))
    bias_new = rows(_rel_bias_tile(p['rel_bias'], rel_new))
    ks, vs, cs, ss, gs = [], [], [], [], []
    y = None
    x = x.reshape(1, n, d)
    for l, lw in enumerate(lws):
        (qa, kaf, vaf, _, _, qg, kg, vg, gg, la, qm, cp, mla) = _inproj(x.reshape(n, d), lw, cos_t, sin_t, tm, F32)
        r3 = lambda a: a.reshape(b, t, a.shape[-1])
        qm4 = qm.reshape(H_C, b, t, 256)
        rope_q = jnp.stack([qm4[hh, :, :, KV_LORA + hh * ROPE_C:KV_LORA + (hh + 1) * ROPE_C] for hh in range(H_C)])
        qmc = jnp.concatenate([qm4[..., 0:KV_LORA], rope_q], axis=-1)
        oa, oc = _dec_attn(l, page_flat, n_pages, r3(qa), qmc, r3(kaf), r3(vaf), r3(mla),
                           bias_last, bias_new, lw['gain'], lw['lam'], lw['wuv'],
                           kt_cache, vt_cache, ct_cache, g_pages)
        og, st = _gla(r3(qg), r3(kg), r3(vg), r3(la), r3(gg), _state_to_blockdiag(state_gla[l]),
                      lw['gla_norm'], t, t, t, F32)
        prev = jnp.repeat(jnp.swapaxes(state_conv[l], 0, 1), t, axis=1)[:, None]
        last = l == len(lws) - 1
        res = _ffn(x, oa.reshape(1, n, -1), og.reshape(1, n, -1), oc.reshape(1, n, -1), lw, tm, prev=prev,
                   final_w=p['final_norm'][None, :] if last else None)
        x, gt = res[0], res[1]
        if last:
            y = res[2]
        ks.append(kaf.reshape(b, t, H_A, 2 * DK_A))
        vs.append(vaf.reshape(b, t, H_A, DV_A))
        cs.append(mla.reshape(b, t, KV_LORA + ROPE_C))
        ss.append(_state_from_blockdiag(st))
        gs.append(gt.reshape(b, t, -1)[:, t - (CONV_W - 1):, :])
    return y.reshape(b, t, d), [jnp.stack(a, axis=0) for a in (ks, vs, cs, ss, gs)]


def kernel(x_prompt, x_sample, cache_diff_k, cache_diff_v, cache_mla, state_gla, state_ffn_conv, page_table,
           attn_norm, w_in, diff_lambda, diff_subln, rel_bias, w_gla_a2, b_gla_a, gla_norm, mla_q_norm, w_uq,
           mla_kv_norm, w_uk, w_uv, w_out, ffn_norm, w_gate, w_up, conv_w, conv_b, w_down, final_norm):
    p = {'attn_norm': attn_norm, 'w_in': w_in, 'diff_lambda': diff_lambda, 'diff_subln': diff_subln,
         'rel_bias': rel_bias, 'w_gla_a2': w_gla_a2, 'b_gla_a': b_gla_a, 'gla_norm': gla_norm,
         'mla_q_norm': mla_q_norm, 'w_uq': w_uq, 'mla_kv_norm': mla_kv_norm, 'w_uk': w_uk, 'w_uv': w_uv,
         'w_out': w_out, 'ffn_norm': ffn_norm, 'w_gate': w_gate, 'w_up': w_up, 'conv_w': conv_w,
         'conv_b': conv_b, 'w_down': w_down, 'final_norm': final_norm}
    lws = [_prep_layer(l, p) for l in range(w_in.shape[0])]
    y_p, new_p = _prompt_trunk(x_prompt, p, lws)
    y_s, new_s = _sample_trunk(x_sample, p, lws, cache_diff_k, cache_diff_v, cache_mla, state_gla,
                               state_ffn_conv, page_table)
    return (y_p, y_s, *new_p, *new_s)
```

```python
import functools
import math

import jax
import jax.numpy as jnp
import numpy as np
from jax import lax
from jax.experimental import pallas as pl
from jax.experimental.pallas import tpu as pltpu

BF = jnp.bfloat16
F32 = jnp.float32

H_A, DK_A, DV_A = 4, 32, 64
H_B, DK_B, DV_B = 4, 64, 128
GATE_RANK, GATE_TAU, GLA_CHUNK = 16, 16.0, 64
H_C, Q_LORA, KV_LORA, NOPE_C, ROPE_C, DV_C = 4, 192, 128, 64, 32, 64
ROPE_THETA = 10000.0
N_BUCKETS, MAX_DISTANCE = 32, 128
NORM_EPS = 1e-6
CONV_W = 3
LOG2E = math.log2(math.e)
ONES_ROWS = 16

WA = H_A * 2 * DK_A
WV = H_A * DV_A
WK_B = H_B * DK_B
WV_B = H_B * DV_B
LANE = 128
VMEM_LIMIT = 56 * 1024 * 1024
ROW_TILE = 256
ATTN_TILE = 512
GLA_ROWS = 512
PAGES_PER_STEP = 16

_C_QA, _C_KA, _C_VA, _C_QG, _C_KG, _C_VG, _C_GG = 0, 256, 512, 768, 1024, 1280, 1792
_C_AG, _C_CQ, _C_CKV, _C_KRA, _C_KRB, _C_END = 2304, 2432, 2688, 2816, 2944, 3072

_NT = (((1,), (1,)), ((), ()))
_TN = (((0,), (0,)), ((), ()))


def _cparams(sem):
    return pltpu.CompilerParams(dimension_semantics=sem, vmem_limit_bytes=VMEM_LIMIT)


def _const_spec(shape):
    nd = len(shape)
    return pl.BlockSpec(shape, lambda *_: (0,) * nd, pipeline_mode=pl.Buffered(1))


def _inproj_kernel(x_ref, nw_ref, w_ref, w2_ref, b2_ref, mqn_ref, wuq_ref, wuk_ref, mkn_ref,
                   cos_ref, sin_ref,
                   qa_ref, kaf_ref, vaf_ref, kab_ref, vab_ref, qg_ref, kg_ref, vg_ref, gg_ref,
                   la_ref, qm_ref, cp_ref, mla_ref):
    x = x_ref[...]
    h = x * lax.rsqrt(jnp.mean(x * x, axis=-1, keepdims=True) + NORM_EPS) * nw_ref[...]
    z = jnp.dot(h.astype(BF), w_ref[...], preferred_element_type=F32)

    qa_ref[...] = (z[:, _C_QA:_C_KA] * (DK_A ** -0.5 * LOG2E)).astype(qa_ref.dtype)
    ka = z[:, _C_KA:_C_VA]
    va = z[:, _C_VA:_C_QG]
    kaf_ref[...] = ka
    vaf_ref[...] = va
    kab_ref[...] = ka.astype(kab_ref.dtype)
    vab_ref[...] = va.astype(vab_ref.dtype)

    qg_ref[...] = z[:, _C_QG:_C_KG] * (DK_B ** -0.5)
    kg_ref[...] = z[:, _C_KG:_C_VG]
    vg_ref[...] = z[:, _C_VG:_C_GG].astype(vg_ref.dtype)
    gg_ref[...] = z[:, _C_GG:_C_AG]
    ag = z[:, _C_AG:_C_CQ]
    xg = jnp.dot(ag.astype(BF), w2_ref[...], preferred_element_type=F32) + b2_ref[...]
    la_ref[...] = (jnp.minimum(xg, 0.0) - jnp.log1p(jnp.exp(-jnp.abs(xg)))) * (1.0 / GATE_TAU)

    cos = cos_ref[...]
    sin = sin_ref[...]
    cq = z[:, _C_CQ:_C_CKV]
    cqn = cq * lax.rsqrt(jnp.sum(cq * cq, axis=-1, keepdims=True) * (1.0 / Q_LORA) + NORM_EPS) * mqn_ref[...]
    qc = jnp.dot(cqn.astype(BF), wuq_ref[...], preferred_element_type=F32)
    scale_c = (NOPE_C + ROPE_C) ** -0.5 * LOG2E
    q_rope = (qc[:, 256:384] * cos + qc[:, 384:512] * sin) * scale_c
    q_lat = jnp.dot(qc[:, 0:256].astype(BF), wuk_ref[...], preferred_element_type=F32) * scale_c
    lane = lax.broadcasted_iota(jnp.int32, q_rope.shape, 1)
    for hh in range(H_C):
        qm_ref[hh, :, 0:KV_LORA] = q_lat[:, hh * KV_LORA:(hh + 1) * KV_LORA].astype(qm_ref.dtype)
        qm_ref[hh, :, KV_LORA:2 * KV_LORA] = jnp.where(lane // ROPE_C == hh, q_rope, 0.0).astype(qm_ref.dtype)
    ckv = z[:, _C_CKV:_C_KRA]
    ckvn = ckv * lax.rsqrt(jnp.mean(ckv * ckv, axis=-1, keepdims=True) + NORM_EPS) * mkn_ref[...]
    krr = z[:, _C_KRA:_C_KRB] * cos + z[:, _C_KRB:_C_END] * sin
    cp_ref[:, 0:KV_LORA] = ckvn.astype(cp_ref.dtype)
    cp_ref[:, KV_LORA:2 * KV_LORA] = krr.astype(cp_ref.dtype)
    mla_ref[:, 0:KV_LORA] = ckvn
    mla_ref[:, KV_LORA:KV_LORA + ROPE_C] = krr[:, 0:ROPE_C]


def _inproj(x, lw, cos_t, sin_t, tm, act_dtype):
    n, d = x.shape
    grid = (n // tm,)
    row = lambda w: pl.BlockSpec((tm, w), lambda i: (i, 0))
    in_specs = [
        row(d), _const_spec((1, d)), _const_spec(lw['w_in'].shape), _const_spec(lw['w2'].shape),
        _const_spec((1, WK_B)), _const_spec((1, 256)), _const_spec(lw['wuq'].shape),
        _const_spec(lw['wuk'].shape), _const_spec((1, KV_LORA)), row(LANE), row(LANE),
    ]
    outs = [
        (WA, act_dtype), (WA, F32), (WV, F32), (WA, act_dtype), (WV, act_dtype),
        (WK_B, F32), (WK_B, F32), (WV_B, act_dtype), (WV_B, F32), (WK_B, F32),
    ]
    out_shape = [jax.ShapeDtypeStruct((n, w), dt) for w, dt in outs]
    out_specs = [row(w) for w, _ in outs]
    out_shape += [jax.ShapeDtypeStruct((H_C, n, 256), act_dtype), jax.ShapeDtypeStruct((n, 256), act_dtype),
                  jax.ShapeDtypeStruct((n, KV_LORA + ROPE_C), F32)]
    out_specs += [pl.BlockSpec((H_C, tm, 256), lambda i: (0, i, 0)), row(256),
                  pl.BlockSpec((tm, KV_LORA + ROPE_C), lambda i: (i, 0))]
    return pl.pallas_call(
        _inproj_kernel, grid=grid, in_specs=in_specs, out_specs=out_specs, out_shape=out_shape,
        compiler_params=_cparams(("parallel",)), name="inproj",
    )(x, lw['attn_norm'], lw['w_in'], lw['w2'], lw['b2'], lw['mqn'], lw['wuq'], lw['wuk'], lw['mkn'],
      cos_t, sin_t)


def _diff_attn_kernel(qi_ref, ki_ref, qT_ref, k_ref, vT_ref, bias_ref, gain_ref, lam_ref, o_ref,
                      qs_ref, m_ref, acc_ref, *, tq):
    p = pl.program_id(1)
    qi = qi_ref[p]
    ki = ki_ref[p]
    nmap = 2 * H_A
    va = DV_A + ONES_ROWS

    @pl.when(ki == 0)
    def _init():
        qT = qT_ref[...]
        row = lax.broadcasted_iota(jnp.int32, qT.shape, 0)
        zero = jnp.zeros_like(qT)
        for hm in range(nmap):
            qs_ref[:, hm * tq:(hm + 1) * tq] = jnp.where(row // DK_A == hm, qT, zero)
        m_ref[...] = jnp.full(m_ref.shape, -jnp.inf, F32)
        acc_ref[...] = jnp.zeros(acc_ref.shape, F32)

    def step(with_bias):
        k = k_ref[...]

        def qk(hm):
            s = jnp.dot(k, qs_ref[:, hm * tq:(hm + 1) * tq], preferred_element_type=F32)
            return s + bias_ref[qi - ki, hm // 2] if with_bias else s

        s_next = qk(0)
        for hm in range(nmap):
            hh = hm // 2
            s = s_next
            if hm + 1 < nmap:
                s_next = qk(hm + 1)
            m_old = m_ref[hm:hm + 1, :]
            m_new = jnp.maximum(m_old, jnp.max(s, axis=0, keepdims=True))
            alpha = jnp.exp2(m_old - m_new)
            pr = jnp.exp2(s - m_new).astype(BF)
            pv = jnp.dot(vT_ref[hh * va:(hh + 1) * va, :], pr, preferred_element_type=F32)
            acc_ref[hm] = alpha * acc_ref[hm] + pv
            m_ref[hm:hm + 1, :] = m_new

    pl.when(qi - ki <= 1)(lambda: step(True))
    pl.when(qi - ki > 1)(lambda: step(False))

    @pl.when(ki == qi)
    def _fin():
        lam = lam_ref[...]
        outs = []
        for hh in range(H_A):
            a1 = acc_ref[2 * hh]
            a2 = acc_ref[2 * hh + 1]
            dh = a1[0:DV_A] / a1[DV_A:DV_A + 1] - lam * (a2[0:DV_A] / a2[DV_A:DV_A + 1])
            ms = jnp.mean(dh * dh, axis=0, keepdims=True)
            outs.append(dh * lax.rsqrt(ms + NORM_EPS))
        out_t = jnp.concatenate(outs, axis=0)
        o_ref[...] = (out_t.T * gain_ref[...]).astype(o_ref.dtype)


def _causal_pairs(nq):
    qi = np.concatenate([np.full(i + 1, i, np.int32) for i in range(nq)])
    ki = np.concatenate([np.arange(i + 1, dtype=np.int32) for i in range(nq)])
    return jnp.asarray(qi), jnp.asarray(ki)


def _diff_attn(q_t, k, v_t, bias_t, gain, lam, tq):
    b, _, t = q_t.shape
    nq = t // tq
    va = DV_A + ONES_ROWS
    qi, ki = _causal_pairs(nq)
    grid_spec = pltpu.PrefetchScalarGridSpec(
        num_scalar_prefetch=2, grid=(b, int(qi.shape[0])),
        in_specs=[
            pl.BlockSpec((None, WA, tq), lambda bb, p, qi, ki: (bb, 0, qi[p])),
            pl.BlockSpec((None, tq, WA), lambda bb, p, qi, ki: (bb, ki[p], 0)),
            pl.BlockSpec((None, H_A * va, tq), lambda bb, p, qi, ki: (bb, 0, ki[p])),
            pl.BlockSpec((2, H_A, tq, tq), lambda bb, p, qi, ki: (0, 0, 0, 0), pipeline_mode=pl.Buffered(1)),
            pl.BlockSpec((1, WV), lambda bb, p, qi, ki: (0, 0)),
            pl.BlockSpec((1, 1), lambda bb, p, qi, ki: (0, 0)),
        ],
        out_specs=pl.BlockSpec((None, tq, WV), lambda bb, p, qi, ki: (bb, qi[p], 0)),
        scratch_shapes=[
            pltpu.VMEM((WA, 2 * H_A * tq), BF), pltpu.VMEM((2 * H_A, tq), F32),
            pltpu.VMEM((2 * H_A, va, tq), F32),
        ],
    )
    return pl.pallas_call(
        functools.partial(_diff_attn_kernel, tq=tq), grid_spec=grid_spec,
        out_shape=jax.ShapeDtypeStruct((b, t, WV), BF),
        compiler_params=_cparams(("parallel", "arbitrary")), name="diff_attn",
    )(qi, ki, q_t, k, v_t, bias_t, gain, lam)


def _mla_attn_kernel(qi_ref, ki_ref, qmT_ref, c_ref, cT_ref, wuvT_ref, o_ref, m_ref, acc_ref, *, tq):
    p = pl.program_id(1)
    qi = qi_ref[p]
    ki = ki_ref[p]

    @pl.when(ki == 0)
    def _init():
        m_ref[...] = jnp.full(m_ref.shape, -jnp.inf, F32)
        acc_ref[...] = jnp.zeros(acc_ref.shape, F32)

    def step(masked):
        c = c_ref[...]
        if masked:
            causal = (lax.broadcasted_iota(jnp.int32, (tq, tq), 0) <= lax.broadcasted_iota(jnp.int32, (tq, tq), 1))

        def qk(hh):
            s = jnp.dot(c, qmT_ref[hh], preferred_element_type=F32)
            return jnp.where(causal, s, -jnp.inf) if masked else s

        s_next = qk(0)
        for hh in range(H_C):
            s = s_next
            if hh + 1 < H_C:
                s_next = qk(hh + 1)
            m_old = m_ref[hh:hh + 1, :]
            m_new = jnp.maximum(m_old, jnp.max(s, axis=0, keepdims=True))
            alpha = jnp.exp2(m_old - m_new)
            pr = jnp.exp2(s - m_new).astype(BF)
            acc_ref[hh] = alpha * acc_ref[hh] + jnp.dot(cT_ref[...], pr, preferred_element_type=F32)
            m_ref[hh:hh + 1, :] = m_new

    pl.when(ki == qi)(lambda: step(True))
    pl.when(ki != qi)(lambda: step(False))

    @pl.when(ki == qi)
    def _fin():
        out_t = jnp.zeros((H_C * DV_C, tq), F32)
        for hh in range(H_C):
            a = acc_ref[hh]
            o_lat = (a[0:KV_LORA] / a[KV_LORA:KV_LORA + 1]).astype(BF)
            out_t = out_t + jnp.dot(wuvT_ref[hh], o_lat, preferred_element_type=F32)
        o_ref[...] = out_t.T.astype(o_ref.dtype)


def _mla_attn(qm_t, c, c_t, wuv_t, tq):
    b, t, _ = c.shape
    nq = t // tq
    qi, ki = _causal_pairs(nq)
    rows = KV_LORA + ONES_ROWS
    grid_spec = pltpu.PrefetchScalarGridSpec(
        num_scalar_prefetch=2, grid=(b, int(qi.shape[0])),
        in_specs=[
            pl.BlockSpec((H_C, None, 256, tq), lambda bb, p, qi, ki: (0, bb, 0, qi[p])),
            pl.BlockSpec((None, tq, 256), lambda bb, p, qi, ki: (bb, ki[p], 0)),
            pl.BlockSpec((None, rows, tq), lambda bb, p, qi, ki: (bb, 0, ki[p])),
            pl.BlockSpec((H_C, H_C * DV_C, KV_LORA), lambda bb, p, qi, ki: (0, 0, 0)),
        ],
        out_specs=pl.BlockSpec((None, tq, H_C * DV_C), lambda bb, p, qi, ki: (bb, qi[p], 0)),
        scratch_shapes=[pltpu.VMEM((H_C, tq), F32), pltpu.VMEM((H_C, rows, tq), F32)],
    )
    return pl.pallas_call(
        functools.partial(_mla_attn_kernel, tq=tq), grid_spec=grid_spec,
        out_shape=jax.ShapeDtypeStruct((b, t, H_C * DV_C), BF),
        compiler_params=_cparams(("parallel", "arbitrary")), name="mla_attn",
    )(qi, ki, qm_t, c, c_t, wuv_t)


def _gla_kernel(q_ref, k_ref, v_ref, g_ref, gate_ref, s0_ref, nw_ref, o_ref, so_ref, st_ref,
                *, chunk, sub, nchunk):
    t = pl.program_id(1)

    @pl.when(t == 0)
    def _init():
        st_ref[...] = s0_ref[...]

    C, c = chunk, sub
    tri = (lax.broadcasted_iota(jnp.int32, (C, C), 0) >= lax.broadcasted_iota(jnp.int32, (C, C), 1)).astype(F32)
    ind = (lax.broadcasted_iota(jnp.int32, (WK_B, WV_B), 0) // DK_B
           == lax.broadcasted_iota(jnp.int32, (WK_B, WV_B), 1) // DV_B).astype(BF)
    bdmask = (lax.broadcasted_iota(jnp.int32, (WV_B, WK_B), 0) // DV_B
              == lax.broadcasted_iota(jnp.int32, (WV_B, WK_B), 1) // DK_B)
    lane_k = lax.broadcasted_iota(jnp.int32, (c, WK_B), 1)
    row_c = lax.broadcasted_iota(jnp.int32, (c, WK_B), 0)
    nw = nw_ref[...]

    def do_chunk(n, carry):
        r = pl.multiple_of(n * C, C)
        q = q_ref[pl.ds(r, C), :]
        k = k_ref[pl.ds(r, C), :]
        v = v_ref[pl.ds(r, C), :].astype(BF)
        g = g_ref[pl.ds(r, C), :]
        b = jnp.dot(tri, g, preferred_element_type=F32, precision=lax.Precision.HIGHEST)
        b_last = b[C - 1:C, :]
        st = st_ref[...]
        qt = (q * jnp.exp(b)).astype(BF)
        o = lax.dot_general(qt, st.astype(BF), _NT, preferred_element_type=F32)
        kt = (k * jnp.exp(b_last - b)).astype(BF)
        u = lax.dot_general(v, kt, _TN, preferred_element_type=F32)
        st_ref[...] = st * jnp.exp(b_last) + jnp.where(bdmask, u, 0.0)
        rows = []
        for blk in range(C // c):
            lo, hi = blk * c, (blk + 1) * c
            b_i, q_i, k_i = b[lo:hi], q[lo:hi], k[lo:hi]
            v_i = v[lo:hi].astype(F32)
            es = []
            for j in range(c):
                e = jnp.exp(jnp.minimum(b_i - b_i[j:j + 1], 0.0)) * q_i * k_i[j:j + 1]
                es.append(jnp.where(row_c >= j, e, 0.0))
            rr = jnp.dot(jnp.concatenate(es, axis=0).astype(BF), ind, preferred_element_type=F32)
            od = rr[0:c] * v_i[0:1]
            for j in range(1, c):
                od = od + rr[j * c:(j + 1) * c] * v_i[j:j + 1]
            if blk > 0:
                b_s = b[lo - 1:lo]
                qh = q_i * jnp.exp(b_i - b_s)
                kh = (k[0:lo] * jnp.exp(b_s - b[0:lo])).astype(BF)
                qs = jnp.concatenate([jnp.where(lane_k // DK_B == hh, qh, 0.0) for hh in range(H_B)], axis=0)
                a = lax.dot_general(qs.astype(BF), kh, _NT, preferred_element_type=F32)
                pv = jnp.dot(a.astype(BF), v[0:lo], preferred_element_type=F32)
                od = od + jnp.concatenate(
                    [pv[hh * c:(hh + 1) * c, hh * DV_B:(hh + 1) * DV_B] for hh in range(H_B)], axis=1)
            rows.append(od)
        o = o + (jnp.concatenate(rows, axis=0) if len(rows) > 1 else rows[0])
        gate = gate_ref[pl.ds(r, C), :]
        outs = []
        for hh in range(H_B):
            oh = o[:, hh * DV_B:(hh + 1) * DV_B]
            outs.append(oh * lax.rsqrt(jnp.mean(oh * oh, axis=-1, keepdims=True) + NORM_EPS) * nw)
        y = jnp.concatenate(outs, axis=1) * (gate * jax.nn.sigmoid(gate))
        o_ref[pl.ds(r, C), :] = y.astype(o_ref.dtype)
        return carry

    if nchunk == 1:
        do_chunk(0, 0)
    else:
        lax.fori_loop(0, nchunk, do_chunk, 0, unroll=2)

    @pl.when(t == pl.num_programs(1) - 1)
    def _fin():
        so_ref[...] = st_ref[...]


def _gla(q, k, v, g, gate, s0t, nw, rows, chunk, sub, out_dtype):
    b, t, _ = q.shape
    nt = t // rows
    blk = lambda w: pl.BlockSpec((None, rows, w), lambda bb, tt: (bb, tt, 0))
    st_spec = pl.BlockSpec((None, WV_B, WK_B), lambda bb, tt: (bb, 0, 0))
    return pl.pallas_call(
        functools.partial(_gla_kernel, chunk=chunk, sub=sub, nchunk=rows // chunk),
        grid=(b, nt),
        in_specs=[blk(WK_B), blk(WK_B), blk(WV_B), blk(WK_B), blk(WV_B), st_spec,
                  pl.BlockSpec((1, DV_B), lambda bb, tt: (0, 0))],
        out_specs=[blk(WV_B), st_spec],
        out_shape=[jax.ShapeDtypeStruct((b, t, WV_B), out_dtype), jax.ShapeDtypeStruct((b, WV_B, WK_B), F32)],
        scratch_shapes=[pltpu.VMEM((WV_B, WK_B), F32)],
        compiler_params=_cparams(("parallel", "arbitrary")), name="gla",
    )(q, k, v, g, gate, s0t, nw)


def _ffn_kernel(*refs, tm, per_seq_rows, with_final):
    it = iter(refs)
    x_ref, oa_ref, og_ref, oc_ref, wo_ref, fnw_ref, wg_ref, wu_ref, cw_ref, cb_ref, wd_ref = (next(it) for _ in range(11))
    prev_ref = next(it) if per_seq_rows else None
    finw_ref = next(it) if with_final else None
    xo_ref, gt_ref = next(it), next(it)
    y_ref = next(it) if with_final else None
    carry_ref = None if per_seq_rows else next(it)

    x = x_ref[...]
    mix = (jnp.dot(oa_ref[...].astype(BF), wo_ref[0:WV, :], preferred_element_type=F32)
           + jnp.dot(og_ref[...].astype(BF), wo_ref[WV:WV + WV_B, :], preferred_element_type=F32)
           + jnp.dot(oc_ref[...].astype(BF), wo_ref[WV + WV_B:, :], preferred_element_type=F32))
    xm = x + mix
    hb = (xm * lax.rsqrt(jnp.mean(xm * xm, axis=-1, keepdims=True) + NORM_EPS) * fnw_ref[...]).astype(BF)
    g = jnp.dot(hb, wg_ref[...], preferred_element_type=F32)
    u = jnp.dot(hb, wu_ref[...], preferred_element_type=F32)
    row = lax.broadcasted_iota(jnp.int32, g.shape, 0)
    g1 = pltpu.roll(g, 1, axis=0)
    g2 = pltpu.roll(g, 2, axis=0)
    if per_seq_rows:
        rm = row % per_seq_rows
        p0, p1 = prev_ref[0], prev_ref[1]
        g1 = jnp.where(rm == 0, p1, g1)
        g2 = jnp.where(rm == 0, p0, jnp.where(rm == 1, p1, g2))
        gt_ref[...] = g
    else:
        @pl.when(pl.program_id(1) == 0)
        def _zero():
            carry_ref[...] = jnp.zeros(carry_ref.shape, F32)
        p0, p1 = carry_ref[6:7, :], carry_ref[7:8, :]
        g1 = jnp.where(row == 0, p1, g1)
        g2 = jnp.where(row == 0, p0, jnp.where(row == 1, p1, g2))
        tail = g[tm - 8:tm, :]
        carry_ref[...] = tail
        gt_ref[...] = tail
    gc = cb_ref[...] + g2 * cw_ref[0:1, :] + g1 * cw_ref[1:2, :] + g * cw_ref[2:3, :]
    act = (gc * jax.nn.sigmoid(gc) * u).astype(BF)
    xo = xm + jnp.dot(act, wd_ref[...], preferred_element_type=F32)
    xo_ref[...] = xo
    if with_final:
        y_ref[...] = xo * lax.rsqrt(jnp.mean(xo * xo, axis=-1, keepdims=True) + NORM_EPS) * finw_ref[...]


def _ffn(x, oa, og, oc, lw, tm, prev=None, final_w=None):
    b, t, d = x.shape
    f = lw['w_gate'].shape[1]
    nt = t // tm
    per_seq = prev is not None
    blk = lambda w: pl.BlockSpec((None, tm, w), lambda bb, tt: (bb, tt, 0))
    in_specs = [blk(d), blk(WV), blk(WV_B), blk(H_C * DV_C), _const_spec(lw['w_out'].shape), _const_spec((1, d)),
                _const_spec(lw['w_gate'].shape), _const_spec(lw['w_up'].shape), _const_spec((CONV_W, f)),
                _const_spec((1, f)), _const_spec(lw['w_down'].shape)]
    args = [x, oa, og, oc, lw['w_out'], lw['ffn_norm'], lw['w_gate'], lw['w_up'], lw['conv_w'], lw['conv_b'],
            lw['w_down']]
    if per_seq:
        in_specs.append(pl.BlockSpec((2, None, tm, f), lambda bb, tt: (0, bb, tt, 0)))
        args.append(prev)
    if final_w is not None:
        in_specs.append(_const_spec((1, d)))
        args.append(final_w)
    gr = tm if per_seq else 8
    out_shape = [jax.ShapeDtypeStruct((b, t, d), F32), jax.ShapeDtypeStruct((b, nt, gr, f), F32)]
    out_specs = [blk(d), pl.BlockSpec((None, None, gr, f), lambda bb, tt: (bb, tt, 0, 0))]
    if final_w is not None:
        out_shape.append(jax.ShapeDtypeStruct((b, t, d), F32))
        out_specs.append(blk(d))
    scratch = [] if per_seq else [pltpu.VMEM((8, f), F32)]
    return pl.pallas_call(
        functools.partial(_ffn_kernel, tm=tm, per_seq_rows=8 if per_seq else 0, with_final=final_w is not None),
        grid=(b, nt), in_specs=in_specs, out_specs=out_specs, out_shape=out_shape, scratch_shapes=scratch,
        compiler_params=_cparams(("parallel", "arbitrary")), name="ffn",
    )(*args)


def _dec_attn_kernel(pt_ref, qa_ref, qm_ref, kn_ref, vn_ref, cn_ref, bl_ref, bn_ref, gain_ref, lam_ref, wuv_ref,
                     kt_hbm, vt_hbm, ct_hbm, oa_ref, oc_ref,
                     kbuf, vbuf, cbuf, sem, qs_ref, qms_ref, kt_ref, vt_ref, ct_ref,
                     ma_ref, la_ref, acca_ref, mc_ref, lc_ref, accc_ref,
                     *, layer, n_pages, g_pages, t_new):
    seq = pl.program_id(0)
    n_seq = pl.num_programs(0)
    ng = n_pages // g_pages
    rc = H_C * t_new
    page = kbuf.shape[-1]
    wc = KV_LORA + ROPE_C

    def group_copies(sq, grp, slot):
        out = []
        for i in range(g_pages):
            pid = pt_ref[sq * n_pages + grp * g_pages + i]
            out.append(pltpu.make_async_copy(kt_hbm.at[layer, pid], kbuf.at[slot, i], sem.at[0, slot]))
            out.append(pltpu.make_async_copy(vt_hbm.at[layer, pid], vbuf.at[slot, i], sem.at[1, slot]))
            out.append(pltpu.make_async_copy(ct_hbm.at[layer, pid], cbuf.at[slot, i], sem.at[2, slot]))
        return out

    def start_group(sq, grp, slot):
        for cp in group_copies(sq, grp, slot):
            cp.start()

    def wait_group(sq, grp, slot):
        for cp in group_copies(sq, grp, slot):
            cp.wait()

    @pl.when(seq == 0)
    def _first():
        start_group(0, 0, 0)

    q = qa_ref[...]
    lane_q = lax.broadcasted_iota(jnp.int32, q.shape, 1)
    qs = jnp.concatenate([jnp.where(lane_q // DK_A == hm, q, 0.0) for hm in range(2 * H_A)], axis=0)
    qs_ref[...] = qs.astype(BF)
    qms_ref[...] = qm_ref[...].reshape(rc, wc).astype(BF)
    ma_ref[...] = jnp.full(ma_ref.shape, -jnp.inf, F32)
    la_ref[...] = jnp.zeros(la_ref.shape, F32)
    acca_ref[...] = jnp.zeros(acca_ref.shape, F32)
    mc_ref[...] = jnp.full(mc_ref.shape, -jnp.inf, F32)
    lc_ref[...] = jnp.zeros(lc_ref.shape, F32)
    accc_ref[...] = jnp.zeros(accc_ref.shape, F32)

    def online(s, m_ref, l_ref, acc_ref, pv):
        m_old = m_ref[...]
        m_new = jnp.maximum(m_old, jnp.max(s, axis=-1, keepdims=True))
        alpha = jnp.exp2(m_old - m_new)
        pr = jnp.exp2(s - m_new)
        l_ref[...] = alpha * l_ref[...] + jnp.sum(pr, axis=-1, keepdims=True)
        acc_ref[...] = alpha * acc_ref[...] + pv(pr.astype(BF))
        m_ref[...] = m_new

    def stage(dst_ref, buf, slot):
        for gg in range(g_pages):
            dst_ref[:, gg * page:(gg + 1) * page] = buf[slot, gg].astype(BF)

    def past(slot, with_bias):
        stage(kt_ref, kbuf, slot)
        s = jnp.dot(qs_ref[...], kt_ref[...], preferred_element_type=F32)
        if with_bias:
            s = s + bl_ref[...]
        stage(ct_ref, cbuf, slot)
        sc = jnp.dot(qms_ref[...], ct_ref[...], preferred_element_type=F32)
        stage(vt_ref, vbuf, slot)
        online(s, ma_ref, la_ref, acca_ref,
               lambda pr: lax.dot_general(pr, vt_ref[...], _NT, preferred_element_type=F32))
        online(sc, mc_ref, lc_ref, accc_ref,
               lambda pr: lax.dot_general(pr, ct_ref[0:KV_LORA, :], _NT, preferred_element_type=F32))

    for grp in range(ng):
        slot = grp % 2
        if grp + 1 < ng:
            start_group(seq, grp + 1, 1 - slot)
        else:
            @pl.when(seq + 1 < n_seq)
            def _prefetch_next_seq():
                start_group(seq + 1, 0, 1 - slot)
        wait_group(seq, grp, slot)
        past(slot, with_bias=grp == ng - 1)

    pad = page - t_new
    kn = jnp.concatenate([kn_ref[...], jnp.zeros((pad, WA), F32)], axis=0).astype(BF)
    vn = jnp.concatenate([vn_ref[...], jnp.zeros((pad, WV), F32)], axis=0).astype(BF)
    cn = jnp.concatenate([cn_ref[...], jnp.zeros((pad, wc), F32)], axis=0).astype(BF)
    s = lax.dot_general(qs_ref[...], kn, _NT, preferred_element_type=F32) + bn_ref[...]
    online(s, ma_ref, la_ref, acca_ref, lambda pr: jnp.dot(pr, vn, preferred_element_type=F32))
    tok = lax.broadcasted_iota(jnp.int32, (rc, page), 0) % t_new
    col = lax.broadcasted_iota(jnp.int32, (rc, page), 1)
    sc = lax.dot_general(qms_ref[...], cn, _NT, preferred_element_type=F32)
    sc = jnp.where(col <= tok, sc, -jnp.inf)
    online(sc, mc_ref, lc_ref, accc_ref, lambda pr: jnp.dot(pr, cn[:, 0:KV_LORA], preferred_element_type=F32))

    lam = lam_ref[...]
    lane = lax.broadcasted_iota(jnp.int32, (t_new, WV), 1)
    o = acca_ref[...] / la_ref[...]
    out = jnp.zeros((t_new, WV), F32)
    for hh in range(H_A):
        a0 = 2 * hh * t_new
        dh = o[a0:a0 + t_new] - lam * o[a0 + t_new:a0 + 2 * t_new]
        hmask = lane // DV_A == hh
        ms = jnp.sum(jnp.where(hmask, dh * dh, 0.0), axis=-1, keepdims=True) * (1.0 / DV_A)
        out = jnp.where(hmask, dh * lax.rsqrt(ms + NORM_EPS), out)
    oa_ref[...] = out * gain_ref[...]
    ol = (accc_ref[...] / lc_ref[...]).astype(BF)
    oc = jnp.zeros((t_new, H_C * DV_C), F32)
    for hh in range(H_C):
        oc = oc + jnp.dot(ol[hh * t_new:(hh + 1) * t_new], wuv_ref[hh], preferred_element_type=F32)
    oc_ref[...] = oc


def _dec_attn(layer, page_flat, n_pages, qa, qm, kn, vn, cn, bias_last, bias_new, gain, lam, wuv,
              kt_cache, vt_cache, ct_cache, g_pages):
    b, t_new, _ = qa.shape
    page = kt_cache.shape[-1]
    ng = n_pages // g_pages
    assert n_pages % g_pages == 0 and ng % 2 == 0, "two-slot page buffer needs an even number of groups"
    ra, rc = 2 * H_A * t_new, H_C * t_new
    wc = KV_LORA + ROPE_C
    seq = lambda w: pl.BlockSpec((None, t_new, w), lambda bb, pt: (bb, 0, 0))
    const = lambda shape: pl.BlockSpec(shape, lambda bb, pt: (0,) * len(shape))
    hbm = pl.BlockSpec(memory_space=pl.ANY)
    in_specs = [seq(WA), pl.BlockSpec((H_C, None, t_new, wc), lambda bb, pt: (0, bb, 0, 0)),
                seq(WA), seq(WV), seq(wc), const((ra, g_pages * page)), const((ra, page)),
                const((1, WV)), const((1, 1)), const((H_C, KV_LORA, H_C * DV_C)), hbm, hbm, hbm]
    grid_spec = pltpu.PrefetchScalarGridSpec(
        num_scalar_prefetch=1, grid=(b,), in_specs=in_specs,
        out_specs=[seq(WV), seq(H_C * DV_C)],
        scratch_shapes=[
            pltpu.VMEM((2, g_pages, WA, page), F32), pltpu.VMEM((2, g_pages, WV, page), F32),
            pltpu.VMEM((2, g_pages, wc, page), F32), pltpu.SemaphoreType.DMA((3, 2)),
            pltpu.VMEM((ra, WA), BF), pltpu.VMEM((rc, wc), BF),
            pltpu.VMEM((WA, g_pages * page), BF), pltpu.VMEM((WV, g_pages * page), BF),
            pltpu.VMEM((wc, g_pages * page), BF),
            pltpu.VMEM((ra, 1), F32), pltpu.VMEM((ra, 1), F32), pltpu.VMEM((ra, WV), F32),
            pltpu.VMEM((rc, 1), F32), pltpu.VMEM((rc, 1), F32), pltpu.VMEM((rc, KV_LORA), F32),
        ],
    )
    return pl.pallas_call(
        functools.partial(_dec_attn_kernel, layer=layer, n_pages=n_pages, g_pages=g_pages, t_new=t_new),
        grid_spec=grid_spec,
        out_shape=[jax.ShapeDtypeStruct((b, t_new, WV), F32), jax.ShapeDtypeStruct((b, t_new, H_C * DV_C), F32)],
        compiler_params=_cparams(("arbitrary",)), name="dec_attn",
    )(page_flat, qa, qm, kn, vn, cn, bias_last, bias_new, gain, lam, wuv, kt_cache, vt_cache, ct_cache)


def _split_points():
    sizes = [WA, WA, WV, WK_B, WK_B, WV_B, WV_B, GATE_RANK, Q_LORA, KV_LORA, ROPE_C]
    return [int(s) for s in np.cumsum(sizes)]


def _pad_cols(a, w):
    return jnp.pad(a, ((0, 0), (0, w - a.shape[1])))


def _prep_layer(l, p):
    sp = _split_points()
    w_in = p['w_in'][l]
    half = ROPE_C // 2
    swap = np.concatenate([np.arange(half, ROPE_C), np.arange(0, half)])
    kr = w_in[:, sp[9]:sp[10]]
    w_cat = jnp.concatenate([
        w_in[:, :sp[6]], _pad_cols(w_in[:, sp[6]:sp[7]], LANE), _pad_cols(w_in[:, sp[7]:sp[8]], 256),
        w_in[:, sp[8]:sp[9]], jnp.tile(kr, (1, H_C)), jnp.tile(kr[:, swap], (1, H_C))], axis=1)
    assert w_cat.shape[1] == _C_END
    w_uq = p['w_uq'][l].reshape(Q_LORA, H_C, NOPE_C + ROPE_C)
    rope_cols = w_uq[:, :, NOPE_C:]
    wuq = jnp.concatenate([w_uq[:, :, :NOPE_C].reshape(Q_LORA, -1), rope_cols.reshape(Q_LORA, -1),
                           rope_cols[:, :, swap].reshape(Q_LORA, -1)], axis=1)
    wuq = jnp.pad(wuq, ((0, 256 - Q_LORA), (0, 0)))
    eye = jnp.eye(H_C, dtype=F32)
    wuk = jnp.einsum('hcn,hg->hngc', p['w_uk'][l], eye).reshape(H_C * NOPE_C, H_C * KV_LORA)
    wuv = jnp.einsum('hcv,hg->hcgv', p['w_uv'][l], eye).reshape(H_C, KV_LORA, H_C * DV_C)
    lam_init = 0.8 - 0.6 * math.exp(-0.3 * l)
    lq1, lk1, lq2, lk2 = p['diff_lambda'][l].astype(F32)
    lam = jnp.exp(jnp.sum(lq1 * lk1)) - jnp.exp(jnp.sum(lq2 * lk2)) + lam_init
    return {
        'attn_norm': p['attn_norm'][l][None, :], 'w_in': w_cat.astype(BF),
        'w2': jnp.pad(p['w_gla_a2'][l], ((0, LANE - GATE_RANK), (0, 0))).astype(BF),
        'b2': p['b_gla_a'][l][None, :], 'mqn': _pad_cols(p['mla_q_norm'][l][None, :], 256),
        'wuq': wuq.astype(BF), 'wuk': wuk.astype(BF), 'mkn': p['mla_kv_norm'][l][None, :],
        'wuv': wuv.astype(BF), 'wuv_t': jnp.swapaxes(wuv, 1, 2).astype(BF), 'gain': jnp.tile(p['diff_subln'][l] * (1.0 - lam_init), H_A)[None, :],
        'lam': lam.reshape(1, 1), 'gla_norm': p['gla_norm'][l][None, :],
        'w_out': p['w_out'][l].astype(BF), 'ffn_norm': p['ffn_norm'][l][None, :],
        'w_gate': p['w_gate'][l].astype(BF), 'w_up': p['w_up'][l].astype(BF),
        'conv_w': p['conv_w'][l], 'conv_b': p['conv_b'][l][None, :], 'w_down': p['w_down'][l].astype(BF),
    }


def _rope_tables(pos):
    half = ROPE_C // 2
    inv = ROPE_THETA ** (-jnp.arange(half, dtype=F32) / half)
    ang = pos.astype(F32)[:, None] * inv
    cos, sin = jnp.cos(ang), jnp.sin(ang)
    cos_t = jnp.tile(jnp.concatenate([cos, cos], axis=1), (1, LANE // ROPE_C))
    sin_t = jnp.tile(jnp.concatenate([-sin, sin], axis=1), (1, LANE // ROPE_C))
    return cos_t, sin_t


def _t5_bucket(rel):
    n = jnp.maximum(rel, 0)
    max_exact = N_BUCKETS // 2
    nf = jnp.maximum(n, max_exact).astype(F32)
    large = max_exact + (jnp.log(nf / max_exact) / math.log(MAX_DISTANCE / max_exact)
                         * (N_BUCKETS - max_exact)).astype(jnp.int32)
    large = jnp.minimum(large, N_BUCKETS - 1)
    return jnp.where(n < max_exact, n, large)


def _rel_bias_tile(rel_table, rel):
    shifted = (rel_table - rel_table[N_BUCKETS - 1]) * LOG2E
    bias = jnp.moveaxis(shifted[_t5_bucket(rel)].astype(F32), -1, 0)
    return jnp.where(rel >= 0, bias, -jnp.inf)


def _toeplitz_bias_t(rel_table, offset, tq):
    ln = 2 * tq
    m = jnp.arange(ln, dtype=jnp.int32)
    m = jnp.where(m < tq, m, m - ln)
    w = _rel_bias_tile(rel_table, offset + m)
    sheared = jnp.tile(w, (1, tq))[:, :tq * (ln - 1)].reshape(w.shape[0], tq, ln - 1)
    return sheared[:, :, :tq]


def _prompt_trunk(x, p, lws):
    b, t, d = x.shape
    n = b * t
    tm = min(ROW_TILE, t)
    tq = min(ATTN_TILE, t)
    assert tq >= MAX_DISTANCE and t % tq == 0 and t % tm == 0 and t % GLA_CHUNK == 0
    gla_rows = min(GLA_ROWS, t)
    pos = jnp.arange(t, dtype=jnp.int32)
    cos_t, sin_t = _rope_tables(jnp.tile(pos, b))
    bias_t = jnp.stack([_toeplitz_bias_t(p['rel_bias'], 0, tq), _toeplitz_bias_t(p['rel_bias'], tq, tq)])
    s0 = jnp.zeros((b, WV_B, WK_B), F32)
    ones_a = jnp.ones((b, H_A, ONES_ROWS, t), BF)
    ones_c = jnp.ones((b, ONES_ROWS, t), BF)
    ks, vs, cs, ss, gs = [], [], [], [], []
    y = None
    for l, lw in enumerate(lws):
        (qa, kaf, vaf, kab, vab, qg, kg, vg, gg, la, qm, cp, mla) = _inproj(x.reshape(n, d), lw, cos_t, sin_t, tm, BF)
        r3 = lambda a: a.reshape(b, t, a.shape[-1])
        q_t = jnp.swapaxes(r3(qa), 1, 2)
        v_t = jnp.concatenate([jnp.swapaxes(r3(vab), 1, 2).reshape(b, H_A, DV_A, t), ones_a], axis=2)
        oa = _diff_attn(q_t, r3(kab), v_t.reshape(b, H_A * (DV_A + ONES_ROWS), t), bias_t, lw['gain'], lw['lam'], tq)
        c3 = r3(cp)
        c_t = jnp.concatenate([jnp.swapaxes(c3[:, :, 0:KV_LORA], 1, 2), ones_c], axis=1)
        oc = _mla_attn(jnp.swapaxes(qm.reshape(H_C, b, t, 256), 2, 3), c3, c_t, lw['wuv_t'], tq)
        og, st = _gla(r3(qg), r3(kg), r3(vg), r3(la), r3(gg), s0, lw['gla_norm'], gla_rows, GLA_CHUNK, 16, BF)
        last = l == len(lws) - 1
        res = _ffn(x, oa, og, oc, lw, tm, final_w=p['final_norm'][None, :] if last else None)
        x, gt = res[0], res[1]
        if last:
            y = res[2]
        ks.append(kaf.reshape(b, t, H_A, 2 * DK_A))
        vs.append(vaf.reshape(b, t, H_A, DV_A))
        cs.append(mla.reshape(b, t, KV_LORA + ROPE_C))
        ss.append(_state_from_blockdiag(st))
        gs.append(gt[:, -1, 8 - (CONV_W - 1):, :])
    return y, [jnp.stack(a, axis=0) for a in (ks, vs, cs, ss, gs)]


def _state_to_blockdiag(s):
    b = s.shape[0]
    eye = jnp.eye(H_B, dtype=s.dtype)
    return jnp.einsum('bhdv,hg->bhvgd', s, eye).reshape(b, WV_B, WK_B)


def _state_from_blockdiag(st):
    b = st.shape[0]
    s5 = st.reshape(b, H_B, DV_B, H_B, DK_B)
    diag = jnp.stack([s5[:, hh, :, hh, :] for hh in range(H_B)], axis=1)
    return jnp.swapaxes(diag, -1, -2)


def _sample_trunk(x, p, lws, cache_k, cache_v, cache_c, state_gla, state_conv, page_table):
    b, t, d = x.shape
    n = b * t
    depth, n_pool, page = cache_k.shape[0], cache_k.shape[1], cache_k.shape[2]
    n_pages = page_table.shape[1]
    past_len = n_pages * page
    tm = min(ROW_TILE, n)
    g_pages = min(PAGES_PER_STEP, n_pages)
    assert n % tm == 0 and tm % t == 0 and n_pages % g_pages == 0 and t == 8 and page == MAX_DISTANCE
    pos = past_len + jnp.arange(t, dtype=jnp.int32)
    cos_t, sin_t = _rope_tables(jnp.tile(pos, b))
    kt_cache = jnp.transpose(cache_k, (0, 1, 3, 4, 2)).reshape(depth, n_pool, WA, page)
    vt_cache = jnp.transpose(cache_v, (0, 1, 3, 4, 2)).reshape(depth, n_pool, WV, page)
    ct_cache = jnp.transpose(cache_c, (0, 1, 3, 2))
    page_flat = page_table.reshape(-1).astype(jnp.int32)
    tok = jnp.arange(t, dtype=jnp.int32)
    col = jnp.arange(page, dtype=jnp.int32)
    rel_last = page + tok[:, None] - col[None, :]
    rel_new = jnp.where(col[None, :] < t, tok[:, None] - col[None, :], -1)
    rows = lambda bias: jnp.broadcast_to(bias[:, None], (H_A, 2, t, page)).reshape(2 * H_A * t, page)
    bias_last = jnp.pad(rows(_rel_bias_tile(p['rel_bias'], rel_last)), ((0, 0), ((g_pages - 1) * page, 0)))
    bias_new = rows(_rel_bias_tile(p['rel_bias'], rel_new))
    ks, vs, cs, ss, gs = [], [], [], [], []
    y = None
    x = x.reshape(1, n, d)
    for l, lw in enumerate(lws):
        (qa, kaf, vaf, _, _, qg, kg, vg, gg, la, qm, cp, mla) = _inproj(x.reshape(n, d), lw, cos_t, sin_t, tm, F32)
        r3 = lambda a: a.reshape(b, t, a.shape[-1])
        qm4 = qm.reshape(H_C, b, t, 256)
        rope_q = jnp.stack([qm4[hh, :, :, KV_LORA + hh * ROPE_C:KV_LORA + (hh + 1) * ROPE_C] for hh in range(H_C)])
        qmc = jnp.concatenate([qm4[..., 0:KV_LORA], rope_q], axis=-1)
        oa, oc = _dec_attn(l, page_flat, n_pages, r3(qa), qmc, r3(kaf), r3(vaf), r3(mla),
                           bias_last, bias_new, lw['gain'], lw['lam'], lw['wuv'],
                           kt_cache, vt_cache, ct_cache, g_pages)
        og, st = _gla(r3(qg), r3(kg), r3(vg), r3(la), r3(gg), _state_to_blockdiag(state_gla[l]),
                      lw['gla_norm'], t, t, t, F32)
        prev = jnp.repeat(jnp.swapaxes(state_conv[l], 0, 1), t, axis=1)[:, None]
        last = l == len(lws) - 1
        res = _ffn(x, oa.reshape(1, n, -1), og.reshape(1, n, -1), oc.reshape(1, n, -1), lw, tm, prev=prev,
                   final_w=p['final_norm'][None, :] if last else None)
        x, gt = res[0], res[1]
        if last:
            y = res[2]
        ks.append(kaf.reshape(b, t, H_A, 2 * DK_A))
        vs.append(vaf.reshape(b, t, H_A, DV_A))
        cs.append(mla.reshape(b, t, KV_LORA + ROPE_C))
        ss.append(_state_from_blockdiag(st))
        gs.append(gt.reshape(b, t, -1)[:, t - (CONV_W - 1):, :])
    return y.reshape(b, t, d), [jnp.stack(a, axis=0) for a in (ks, vs, cs, ss, gs)]


def kernel(x_prompt, x_sample, cache_diff_k, cache_diff_v, cache_mla, state_gla, state_ffn_conv, page_table,
           attn_norm, w_in, diff_lambda, diff_subln, rel_bias, w_gla_a2, b_gla_a, gla_norm, mla_q_norm, w_uq,
           mla_kv_norm, w_uk, w_uv, w_out, ffn_norm, w_gate, w_up, conv_w, conv_b, w_down, final_norm):
    p = {'attn_norm': attn_norm, 'w_in': w_in, 'diff_lambda': diff_lambda, 'diff_subln': diff_subln,
         'rel_bias': rel_bias, 'w_gla_a2': w_gla_a2, 'b_gla_a': b_gla_a, 'gla_norm': gla_norm,
         'mla_q_norm': mla_q_norm, 'w_uq': w_uq, 'mla_kv_norm': mla_kv_norm, 'w_uk': w_uk, 'w_uv': w_uv,
         'w_out': w_out, 'ffn_norm': ffn_norm, 'w_gate': w_gate, 'w_up': w_up, 'conv_w': conv_w,
         'conv_b': conv_b, 'w_down': w_down, 'final_norm': final_norm}
    lws = [_prep_layer(l, p) for l in range(w_in.shape[0])]
    y_s, new_s = _sample_trunk(x_sample, p, lws, cache_diff_k, cache_diff_v, cache_mla, state_gla,
                               state_ffn_conv, page_table)
    y_p, new_p = _prompt_trunk(x_prompt, p, lws)
    return (y_p, y_s, *new_p, *new_s)
```

```python
import functools
import math

import jax
import jax.numpy as jnp
import numpy as np
from jax import lax
from jax.experimental import pallas as pl
from jax.experimental.pallas import tpu as pltpu

BF = jnp.bfloat16
F32 = jnp.float32

H_A, DK_A, DV_A = 4, 32, 64
H_B, DK_B, DV_B = 4, 64, 128
GATE_RANK, GATE_TAU, GLA_CHUNK = 16, 16.0, 64
H_C, Q_LORA, KV_LORA, NOPE_C, ROPE_C, DV_C = 4, 192, 128, 64, 32, 64
ROPE_THETA = 10000.0
N_BUCKETS, MAX_DISTANCE = 32, 128
NORM_EPS = 1e-6
CONV_W = 3
LOG2E = math.log2(math.e)
ONES_ROWS = 16

WA = H_A * 2 * DK_A
WV = H_A * DV_A
WK_B = H_B * DK_B
WV_B = H_B * DV_B
LANE = 128
VMEM_LIMIT = 56 * 1024 * 1024
ROW_TILE = 256
ATTN_TILE = 512
GLA_ROWS = 512
PAGES_PER_STEP = 16

_C_QA, _C_KA, _C_VA, _C_QG, _C_KG, _C_VG, _C_GG = 0, 256, 512, 768, 1024, 1280, 1792
_C_AG, _C_CQ, _C_CKV, _C_KRA, _C_KRB, _C_END = 2304, 2432, 2688, 2816, 2944, 3072

_NT = (((1,), (1,)), ((), ()))
_TN = (((0,), (0,)), ((), ()))


def _cparams(sem):
    return pltpu.CompilerParams(dimension_semantics=sem, vmem_limit_bytes=VMEM_LIMIT)


def _const_spec(shape):
    nd = len(shape)
    return pl.BlockSpec(shape, lambda *_: (0,) * nd, pipeline_mode=pl.Buffered(1))


def _inproj_kernel(x_ref, nw_ref, w_ref, w2_ref, b2_ref, mqn_ref, wuq_ref, wuk_ref, mkn_ref,
                   cos_ref, sin_ref,
                   qa_ref, kaf_ref, vaf_ref, kab_ref, vab_ref, qg_ref, kg_ref, vg_ref, gg_ref,
                   la_ref, qm_ref, cp_ref, mla_ref):
    x = x_ref[...]
    h = x * lax.rsqrt(jnp.mean(x * x, axis=-1, keepdims=True) + NORM_EPS) * nw_ref[...]
    z = jnp.dot(h.astype(BF), w_ref[...], preferred_element_type=F32)

    qa_ref[...] = (z[:, _C_QA:_C_KA] * (DK_A ** -0.5 * LOG2E)).astype(qa_ref.dtype)
    ka = z[:, _C_KA:_C_VA]
    va = z[:, _C_VA:_C_QG]
    kaf_ref[...] = ka
    vaf_ref[...] = va
    kab_ref[...] = ka.astype(kab_ref.dtype)
    vab_ref[...] = va.astype(vab_ref.dtype)

    qg_ref[...] = z[:, _C_QG:_C_KG] * (DK_B ** -0.5)
    kg_ref[...] = z[:, _C_KG:_C_VG]
    vg_ref[...] = z[:, _C_VG:_C_GG].astype(vg_ref.dtype)
    gg_ref[...] = z[:, _C_GG:_C_AG]
    ag = z[:, _C_AG:_C_CQ]
    xg = jnp.dot(ag.astype(BF), w2_ref[...], preferred_element_type=F32) + b2_ref[...]
    la_ref[...] = (jnp.minimum(xg, 0.0) - jnp.log1p(jnp.exp(-jnp.abs(xg)))) * (1.0 / GATE_TAU)

    cos = cos_ref[...]
    sin = sin_ref[...]
    cq = z[:, _C_CQ:_C_CKV]
    cqn = cq * lax.rsqrt(jnp.sum(cq * cq, axis=-1, keepdims=True) * (1.0 / Q_LORA) + NORM_EPS) * mqn_ref[...]
    qc = jnp.dot(cqn.astype(BF), wuq_ref[...], preferred_element_type=F32)
    scale_c = (NOPE_C + ROPE_C) ** -0.5 * LOG2E
    q_rope = (qc[:, 256:384] * cos + qc[:, 384:512] * sin) * scale_c
    q_lat = jnp.dot(qc[:, 0:256].astype(BF), wuk_ref[...], preferred_element_type=F32) * scale_c
    lane = lax.broadcasted_iota(jnp.int32, q_rope.shape, 1)
    for hh in range(H_C):
        qm_ref[hh, :, 0:KV_LORA] = q_lat[:, hh * KV_LORA:(hh + 1) * KV_LORA].astype(qm_ref.dtype)
        qm_ref[hh, :, KV_LORA:2 * KV_LORA] = jnp.where(lane // ROPE_C == hh, q_rope, 0.0).astype(qm_ref.dtype)
    ckv = z[:, _C_CKV:_C_KRA]
    ckvn = ckv * lax.rsqrt(jnp.mean(ckv * ckv, axis=-1, keepdims=True) + NORM_EPS) * mkn_ref[...]
    krr = z[:, _C_KRA:_C_KRB] * cos + z[:, _C_KRB:_C_END] * sin
    cp_ref[:, 0:KV_LORA] = ckvn.astype(cp_ref.dtype)
    cp_ref[:, KV_LORA:2 * KV_LORA] = krr.astype(cp_ref.dtype)
    mla_ref[:, 0:KV_LORA] = ckvn
    mla_ref[:, KV_LORA:KV_LORA + ROPE_C] = krr[:, 0:ROPE_C]


def _inproj(x, lw, cos_t, sin_t, tm, act_dtype):
    n, d = x.shape
    grid = (n // tm,)
    row = lambda w: pl.BlockSpec((tm, w), lambda i: (i, 0))
    in_specs = [
        row(d), _const_spec((1, d)), _const_spec(lw['w_in'].shape), _const_spec(lw['w2'].shape),
        _const_spec((1, WK_B)), _const_spec((1, 256)), _const_spec(lw['wuq'].shape),
        _const_spec(lw['wuk'].shape), _const_spec((1, KV_LORA)), row(LANE), row(LANE),
    ]
    outs = [
        (WA, act_dtype), (WA, F32), (WV, F32), (WA, act_dtype), (WV, act_dtype),
        (WK_B, F32), (WK_B, F32), (WV_B, act_dtype), (WV_B, F32), (WK_B, F32),
    ]
    out_shape = [jax.ShapeDtypeStruct((n, w), dt) for w, dt in outs]
    out_specs = [row(w) for w, _ in outs]
    out_shape += [jax.ShapeDtypeStruct((H_C, n, 256), act_dtype), jax.ShapeDtypeStruct((n, 256), act_dtype),
                  jax.ShapeDtypeStruct((n, KV_LORA + ROPE_C), F32)]
    out_specs += [pl.BlockSpec((H_C, tm, 256), lambda i: (0, i, 0)), row(256),
                  pl.BlockSpec((tm, KV_LORA + ROPE_C), lambda i: (i, 0))]
    return pl.pallas_call(
        _inproj_kernel, grid=grid, in_specs=in_specs, out_specs=out_specs, out_shape=out_shape,
        compiler_params=_cparams(("parallel",)), name="inproj",
    )(x, lw['attn_norm'], lw['w_in'], lw['w2'], lw['b2'], lw['mqn'], lw['wuq'], lw['wuk'], lw['mkn'],
      cos_t, sin_t)


def _diff_attn_kernel(qi_ref, ki_ref, qT_ref, k_ref, vT_ref, bias_ref, gain_ref, lam_ref, o_ref,
                      qs_ref, m_ref, acc_ref, *, tq):
    p = pl.program_id(1)
    qi = qi_ref[p]
    ki = ki_ref[p]
    nmap = 2 * H_A
    va = DV_A + ONES_ROWS

    @pl.when(ki == 0)
    def _init():
        qT = qT_ref[...]
        row = lax.broadcasted_iota(jnp.int32, qT.shape, 0)
        zero = jnp.zeros_like(qT)
        for hm in range(nmap):
            qs_ref[:, hm * tq:(hm + 1) * tq] = jnp.where(row // DK_A == hm, qT, zero)
        m_ref[...] = jnp.full(m_ref.shape, -jnp.inf, F32)
        acc_ref[...] = jnp.zeros(acc_ref.shape, F32)

    def step(with_bias):
        k = k_ref[...]

        def qk(hm):
            s = jnp.dot(k, qs_ref[:, hm * tq:(hm + 1) * tq], preferred_element_type=F32)
            return s + bias_ref[qi - ki, hm // 2] if with_bias else s

        s_next = qk(0)
        for hm in range(nmap):
            hh = hm // 2
            s = s_next
            if hm + 1 < nmap:
                s_next = qk(hm + 1)
            m_old = m_ref[hm:hm + 1, :]
            m_new = jnp.maximum(m_old, jnp.max(s, axis=0, keepdims=True))
            alpha = jnp.exp2(m_old - m_new)
            pr = jnp.exp2(s - m_new).astype(BF)
            pv = jnp.dot(vT_ref[hh * va:(hh + 1) * va, :], pr, preferred_element_type=F32)
            acc_ref[hm] = alpha * acc_ref[hm] + pv
            m_ref[hm:hm + 1, :] = m_new

    pl.when(qi - ki <= 1)(lambda: step(True))
    pl.when(qi - ki > 1)(lambda: step(False))

    @pl.when(ki == qi)
    def _fin():
        lam = lam_ref[...]
        outs = []
        for hh in range(H_A):
            a1 = acc_ref[2 * hh]
            a2 = acc_ref[2 * hh + 1]
            dh = a1[0:DV_A] / a1[DV_A:DV_A + 1] - lam * (a2[0:DV_A] / a2[DV_A:DV_A + 1])
            ms = jnp.mean(dh * dh, axis=0, keepdims=True)
            outs.append(dh * lax.rsqrt(ms + NORM_EPS))
        out_t = jnp.concatenate(outs, axis=0)
        o_ref[...] = (out_t.T * gain_ref[...]).astype(o_ref.dtype)


def _causal_pairs(nq):
    qi = np.concatenate([np.full(i + 1, i, np.int32) for i in range(nq)])
    ki = np.concatenate([np.arange(i + 1, dtype=np.int32) for i in range(nq)])
    return jnp.asarray(qi), jnp.asarray(ki)


def _diff_attn(q_t, k, v_t, bias_t, gain, lam, tq):
    b, _, t = q_t.shape
    nq = t // tq
    va = DV_A + ONES_ROWS
    qi, ki = _causal_pairs(nq)
    grid_spec = pltpu.PrefetchScalarGridSpec(
        num_scalar_prefetch=2, grid=(b, int(qi.shape[0])),
        in_specs=[
            pl.BlockSpec((None, WA, tq), lambda bb, p, qi, ki: (bb, 0, qi[p])),
            pl.BlockSpec((None, tq, WA), lambda bb, p, qi, ki: (bb, ki[p], 0)),
            pl.BlockSpec((None, H_A * va, tq), lambda bb, p, qi, ki: (bb, 0, ki[p])),
            pl.BlockSpec((2, H_A, tq, tq), lambda bb, p, qi, ki: (0, 0, 0, 0), pipeline_mode=pl.Buffered(1)),
            pl.BlockSpec((1, WV), lambda bb, p, qi, ki: (0, 0)),
            pl.BlockSpec((1, 1), lambda bb, p, qi, ki: (0, 0)),
        ],
        out_specs=pl.BlockSpec((None, tq, WV), lambda bb, p, qi, ki: (bb, qi[p], 0)),
        scratch_shapes=[
            pltpu.VMEM((WA, 2 * H_A * tq), BF), pltpu.VMEM((2 * H_A, tq), F32),
            pltpu.VMEM((2 * H_A, va, tq), F32),
        ],
    )
    return pl.pallas_call(
        functools.partial(_diff_attn_kernel, tq=tq), grid_spec=grid_spec,
        out_shape=jax.ShapeDtypeStruct((b, t, WV), BF),
        compiler_params=_cparams(("parallel", "arbitrary")), name="diff_attn",
    )(qi, ki, q_t, k, v_t, bias_t, gain, lam)


def _mla_attn_kernel(qi_ref, ki_ref, qmT_ref, c_ref, cT_ref, wuvT_ref, o_ref, m_ref, acc_ref, *, tq):
    p = pl.program_id(1)
    qi = qi_ref[p]
    ki = ki_ref[p]

    @pl.when(ki == 0)
    def _init():
        m_ref[...] = jnp.full(m_ref.shape, -jnp.inf, F32)
        acc_ref[...] = jnp.zeros(acc_ref.shape, F32)

    def step(masked):
        c = c_ref[...]
        if masked:
            causal = (lax.broadcasted_iota(jnp.int32, (tq, tq), 0) <= lax.broadcasted_iota(jnp.int32, (tq, tq), 1))

        def qk(hh):
            s = jnp.dot(c, qmT_ref[hh], preferred_element_type=F32)
            return jnp.where(causal, s, -jnp.inf) if masked else s

        s_next = qk(0)
        for hh in range(H_C):
            s = s_next
            if hh + 1 < H_C:
                s_next = qk(hh + 1)
            m_old = m_ref[hh:hh + 1, :]
            m_new = jnp.maximum(m_old, jnp.max(s, axis=0, keepdims=True))
            alpha = jnp.exp2(m_old - m_new)
            pr = jnp.exp2(s - m_new).astype(BF)
            acc_ref[hh] = alpha * acc_ref[hh] + jnp.dot(cT_ref[...], pr, preferred_element_type=F32)
            m_ref[hh:hh + 1, :] = m_new

    pl.when(ki == qi)(lambda: step(True))
    pl.when(ki != qi)(lambda: step(False))

    @pl.when(ki == qi)
    def _fin():
        out_t = jnp.zeros((H_C * DV_C, tq), F32)
        for hh in range(H_C):
            a = acc_ref[hh]
            o_lat = (a[0:KV_LORA] / a[KV_LORA:KV_LORA + 1]).astype(BF)
            out_t = out_t + jnp.dot(wuvT_ref[hh], o_lat, preferred_element_type=F32)
        o_ref[...] = out_t.T.astype(o_ref.dtype)


def _mla_attn(qm_t, c, c_t, wuv_t, tq):
    b, t, _ = c.shape
    nq = t // tq
    qi, ki = _causal_pairs(nq)
    rows = KV_LORA + ONES_ROWS
    grid_spec = pltpu.PrefetchScalarGridSpec(
        num_scalar_prefetch=2, grid=(b, int(qi.shape[0])),
        in_specs=[
            pl.BlockSpec((H_C, None, 256, tq), lambda bb, p, qi, ki: (0, bb, 0, qi[p])),
            pl.BlockSpec((None, tq, 256), lambda bb, p, qi, ki: (bb, ki[p], 0)),
            pl.BlockSpec((None, rows, tq), lambda bb, p, qi, ki: (bb, 0, ki[p])),
            pl.BlockSpec((H_C, H_C * DV_C, KV_LORA), lambda bb, p, qi, ki: (0, 0, 0)),
        ],
        out_specs=pl.BlockSpec((None, tq, H_C * DV_C), lambda bb, p, qi, ki: (bb, qi[p], 0)),
        scratch_shapes=[pltpu.VMEM((H_C, tq), F32), pltpu.VMEM((H_C, rows, tq), F32)],
    )
    return pl.pallas_call(
        functools.partial(_mla_attn_kernel, tq=tq), grid_spec=grid_spec,
        out_shape=jax.ShapeDtypeStruct((b, t, H_C * DV_C), BF),
        compiler_params=_cparams(("parallel", "arbitrary")), name="mla_attn",
    )(qi, ki, qm_t, c, c_t, wuv_t)


def _gla_kernel(q_ref, k_ref, v_ref, g_ref, gate_ref, s0_ref, nw_ref, o_ref, so_ref, st_ref,
                *, chunk, sub, nchunk):
    t = pl.program_id(1)

    @pl.when(t == 0)
    def _init():
        st_ref[...] = s0_ref[...]

    C, c = chunk, sub
    tri = (lax.broadcasted_iota(jnp.int32, (C, C), 0) >= lax.broadcasted_iota(jnp.int32, (C, C), 1)).astype(F32)
    ind = (lax.broadcasted_iota(jnp.int32, (WK_B, WV_B), 0) // DK_B
           == lax.broadcasted_iota(jnp.int32, (WK_B, WV_B), 1) // DV_B).astype(BF)
    bdmask = (lax.broadcasted_iota(jnp.int32, (WV_B, WK_B), 0) // DV_B
              == lax.broadcasted_iota(jnp.int32, (WV_B, WK_B), 1) // DK_B)
    lane_k = lax.broadcasted_iota(jnp.int32, (c, WK_B), 1)
    row_c = lax.broadcasted_iota(jnp.int32, (c, WK_B), 0)
    nw = nw_ref[...]

    def do_chunk(n, carry):
        r = pl.multiple_of(n * C, C)
        q = q_ref[pl.ds(r, C), :]
        k = k_ref[pl.ds(r, C), :]
        v = v_ref[pl.ds(r, C), :].astype(BF)
        g = g_ref[pl.ds(r, C), :]
        b = jnp.dot(tri, g, preferred_element_type=F32, precision=lax.Precision.HIGHEST)
        b_last = b[C - 1:C, :]
        st = st_ref[...]
        qt = (q * jnp.exp(b)).astype(BF)
        o = lax.dot_general(qt, st.astype(BF), _NT, preferred_element_type=F32)
        kt = (k * jnp.exp(b_last - b)).astype(BF)
        u = lax.dot_general(v, kt, _TN, preferred_element_type=F32)
        st_ref[...] = st * jnp.exp(b_last) + jnp.where(bdmask, u, 0.0)
        rows = []
        for blk in range(C // c):
            lo, hi = blk * c, (blk + 1) * c
            b_i, q_i, k_i = b[lo:hi], q[lo:hi], k[lo:hi]
            v_i = v[lo:hi].astype(F32)
            es = []
            for j in range(c):
                e = jnp.exp(jnp.minimum(b_i - b_i[j:j + 1], 0.0)) * q_i * k_i[j:j + 1]
                es.append(jnp.where(row_c >= j, e, 0.0))
            rr = jnp.dot(jnp.concatenate(es, axis=0).astype(BF), ind, preferred_element_type=F32)
            od = rr[0:c] * v_i[0:1]
            for j in range(1, c):
                od = od + rr[j * c:(j + 1) * c] * v_i[j:j + 1]
            if blk > 0:
                b_s = b[lo - 1:lo]
                qh = q_i * jnp.exp(b_i - b_s)
                kh = (k[0:lo] * jnp.exp(b_s - b[0:lo])).astype(BF)
                qs = jnp.concatenate([jnp.where(lane_k // DK_B == hh, qh, 0.0) for hh in range(H_B)], axis=0)
                a = lax.dot_general(qs.astype(BF), kh, _NT, preferred_element_type=F32)
                pv = jnp.dot(a.astype(BF), v[0:lo], preferred_element_type=F32)
                od = od + jnp.concatenate(
                    [pv[hh * c:(hh + 1) * c, hh * DV_B:(hh + 1) * DV_B] for hh in range(H_B)], axis=1)
            rows.append(od)
        o = o + (jnp.concatenate(rows, axis=0) if len(rows) > 1 else rows[0])
        gate = gate_ref[pl.ds(r, C), :]
        outs = []
        for hh in range(H_B):
            oh = o[:, hh * DV_B:(hh + 1) * DV_B]
            outs.append(oh * lax.rsqrt(jnp.mean(oh * oh, axis=-1, keepdims=True) + NORM_EPS) * nw)
        y = jnp.concatenate(outs, axis=1) * (gate * jax.nn.sigmoid(gate))
        o_ref[pl.ds(r, C), :] = y.astype(o_ref.dtype)
        return carry

    if nchunk == 1:
        do_chunk(0, 0)
    else:
        lax.fori_loop(0, nchunk, do_chunk, 0, unroll=2)

    @pl.when(t == pl.num_programs(1) - 1)
    def _fin():
        so_ref[...] = st_ref[...]


def _gla(q, k, v, g, gate, s0t, nw, rows, chunk, sub, out_dtype):
    b, t, _ = q.shape
    nt = t // rows
    blk = lambda w: pl.BlockSpec((None, rows, w), lambda bb, tt: (bb, tt, 0))
    st_spec = pl.BlockSpec((None, WV_B, WK_B), lambda bb, tt: (bb, 0, 0))
    return pl.pallas_call(
        functools.partial(_gla_kernel, chunk=chunk, sub=sub, nchunk=rows // chunk),
        grid=(b, nt),
        in_specs=[blk(WK_B), blk(WK_B), blk(WV_B), blk(WK_B), blk(WV_B), st_spec,
                  pl.BlockSpec((1, DV_B), lambda bb, tt: (0, 0))],
        out_specs=[blk(WV_B), st_spec],
        out_shape=[jax.ShapeDtypeStruct((b, t, WV_B), out_dtype), jax.ShapeDtypeStruct((b, WV_B, WK_B), F32)],
        scratch_shapes=[pltpu.VMEM((WV_B, WK_B), F32)],
        compiler_params=_cparams(("parallel", "arbitrary")), name="gla",
    )(q, k, v, g, gate, s0t, nw)


def _ffn_kernel(*refs, tm, per_seq_rows, with_final):
    it = iter(refs)
    x_ref, oa_ref, og_ref, oc_ref, wo_ref, fnw_ref, wg_ref, wu_ref, cw_ref, cb_ref, wd_ref = (next(it) for _ in range(11))
    prev_ref = next(it) if per_seq_rows else None
    finw_ref = next(it) if with_final else None
    xo_ref, gt_ref = next(it), next(it)
    y_ref = next(it) if with_final else None
    carry_ref = None if per_seq_rows else next(it)

    x = x_ref[...]
    mix = (jnp.dot(oa_ref[...].astype(BF), wo_ref[0:WV, :], preferred_element_type=F32)
           + jnp.dot(og_ref[...].astype(BF), wo_ref[WV:WV + WV_B, :], preferred_element_type=F32)
           + jnp.dot(oc_ref[...].astype(BF), wo_ref[WV + WV_B:, :], preferred_element_type=F32))
    xm = x + mix
    hb = (xm * lax.rsqrt(jnp.mean(xm * xm, axis=-1, keepdims=True) + NORM_EPS) * fnw_ref[...]).astype(BF)
    g = jnp.dot(hb, wg_ref[...], preferred_element_type=F32)
    u = jnp.dot(hb, wu_ref[...], preferred_element_type=F32)
    row = lax.broadcasted_iota(jnp.int32, g.shape, 0)
    g1 = pltpu.roll(g, 1, axis=0)
    g2 = pltpu.roll(g, 2, axis=0)
    if per_seq_rows:
        rm = row % per_seq_rows
        p0, p1 = prev_ref[0], prev_ref[1]
        g1 = jnp.where(rm == 0, p1, g1)
        g2 = jnp.where(rm == 0, p0, jnp.where(rm == 1, p1, g2))
        gt_ref[...] = g
    else:
        @pl.when(pl.program_id(1) == 0)
        def _zero():
            carry_ref[...] = jnp.zeros(carry_ref.shape, F32)
        p0, p1 = carry_ref[6:7, :], carry_ref[7:8, :]
        g1 = jnp.where(row == 0, p1, g1)
        g2 = jnp.where(row == 0, p0, jnp.where(row == 1, p1, g2))
        tail = g[tm - 8:tm, :]
        carry_ref[...] = tail
        gt_ref[...] = tail
    gc = cb_ref[...] + g2 * cw_ref[0:1, :] + g1 * cw_ref[1:2, :] + g * cw_ref[2:3, :]
    act = (gc * jax.nn.sigmoid(gc) * u).astype(BF)
    xo = xm + jnp.dot(act, wd_ref[...], preferred_element_type=F32)
    xo_ref[...] = xo
    if with_final:
        y_ref[...] = xo * lax.rsqrt(jnp.mean(xo * xo, axis=-1, keepdims=True) + NORM_EPS) * finw_ref[...]


def _ffn(x, oa, og, oc, lw, tm, prev=None, final_w=None):
    b, t, d = x.shape
    f = lw['w_gate'].shape[1]
    nt = t // tm
    per_seq = prev is not None
    blk = lambda w: pl.BlockSpec((None, tm, w), lambda bb, tt: (bb, tt, 0))
    in_specs = [blk(d), blk(WV), blk(WV_B), blk(H_C * DV_C), _const_spec(lw['w_out'].shape), _const_spec((1, d)),
                _const_spec(lw['w_gate'].shape), _const_spec(lw['w_up'].shape), _const_spec((CONV_W, f)),
                _const_spec((1, f)), _const_spec(lw['w_down'].shape)]
    args = [x, oa, og, oc, lw['w_out'], lw['ffn_norm'], lw['w_gate'], lw['w_up'], lw['conv_w'], lw['conv_b'],
            lw['w_down']]
    if per_seq:
        in_specs.append(pl.BlockSpec((2, None, tm, f), lambda bb, tt: (0, bb, tt, 0)))
        args.append(prev)
    if final_w is not None:
        in_specs.append(_const_spec((1, d)))
        args.append(final_w)
    gr = tm if per_seq else 8
    out_shape = [jax.ShapeDtypeStruct((b, t, d), F32), jax.ShapeDtypeStruct((b, nt, gr, f), F32)]
    out_specs = [blk(d), pl.BlockSpec((None, None, gr, f), lambda bb, tt: (bb, tt, 0, 0))]
    if final_w is not None:
        out_shape.append(jax.ShapeDtypeStruct((b, t, d), F32))
        out_specs.append(blk(d))
    scratch = [] if per_seq else [pltpu.VMEM((8, f), F32)]
    return pl.pallas_call(
        functools.partial(_ffn_kernel, tm=tm, per_seq_rows=8 if per_seq else 0, with_final=final_w is not None),
        grid=(b, nt), in_specs=in_specs, out_specs=out_specs, out_shape=out_shape, scratch_shapes=scratch,
        compiler_params=_cparams(("parallel", "arbitrary")), name="ffn",
    )(*args)


def _dec_attn_kernel(pt_ref, qa_ref, qm_ref, kn_ref, vn_ref, cn_ref, bl_ref, bn_ref, gain_ref, lam_ref, wuv_ref,
                     kt_hbm, vt_hbm, ct_hbm, oa_ref, oc_ref,
                     kbuf, vbuf, cbuf, sem, qs_ref, qms_ref, kt_ref, vt_ref, ct_ref,
                     ma_ref, la_ref, acca_ref, mc_ref, lc_ref, accc_ref,
                     *, layer, n_pages, g_pages, t_new):
    seq = pl.program_id(0)
    n_seq = pl.num_programs(0)
    ng = n_pages // g_pages
    rc = H_C * t_new
    page = kbuf.shape[-1]
    wc = KV_LORA + ROPE_C

    streams = ((kt_hbm, kbuf), (ct_hbm, cbuf), (vt_hbm, vbuf))

    def stream_copies(kind, sq, grp, slot):
        hbm, buf = streams[kind]
        return [pltpu.make_async_copy(hbm.at[layer, pt_ref[sq * n_pages + grp * g_pages + i]], buf.at[slot, i],
                                      sem.at[kind, slot]) for i in range(g_pages)]

    def start_group(sq, grp, slot):
        for kind in range(len(streams)):
            for i, cp in enumerate(stream_copies(kind, sq, grp, slot)):
                cp.start(priority=i % 2)

    def wait_stream(kind, sq, grp, slot):
        for cp in stream_copies(kind, sq, grp, slot):
            cp.wait()

    @pl.when(seq == 0)
    def _first():
        start_group(0, 0, 0)

    q = qa_ref[...]
    lane_q = lax.broadcasted_iota(jnp.int32, q.shape, 1)
    qs = jnp.concatenate([jnp.where(lane_q // DK_A == hm, q, 0.0) for hm in range(2 * H_A)], axis=0)
    qs_ref[...] = qs.astype(BF)
    qms_ref[...] = qm_ref[...].reshape(rc, wc).astype(BF)
    ma_ref[...] = jnp.full(ma_ref.shape, -jnp.inf, F32)
    la_ref[...] = jnp.zeros(la_ref.shape, F32)
    acca_ref[...] = jnp.zeros(acca_ref.shape, F32)
    mc_ref[...] = jnp.full(mc_ref.shape, -jnp.inf, F32)
    lc_ref[...] = jnp.zeros(lc_ref.shape, F32)
    accc_ref[...] = jnp.zeros(accc_ref.shape, F32)

    def online(s, m_ref, l_ref, acc_ref, pv):
        m_old = m_ref[...]
        m_new = jnp.maximum(m_old, jnp.max(s, axis=-1, keepdims=True))
        alpha = jnp.exp2(m_old - m_new)
        pr = jnp.exp2(s - m_new)
        l_ref[...] = alpha * l_ref[...] + jnp.sum(pr, axis=-1, keepdims=True)
        acc_ref[...] = alpha * acc_ref[...] + pv(pr.astype(BF))
        m_ref[...] = m_new

    def stage(dst_ref, buf, slot):
        for gg in range(g_pages):
            dst_ref[:, gg * page:(gg + 1) * page] = buf[slot, gg].astype(BF)

    def past(grp, slot, with_bias):
        for kind in range(len(streams)):
            wait_stream(kind, seq, grp, slot)
        stage(kt_ref, kbuf, slot)
        s = jnp.dot(qs_ref[...], kt_ref[...], preferred_element_type=F32)
        if with_bias:
            s = s + bl_ref[...]
        stage(ct_ref, cbuf, slot)
        sc = jnp.dot(qms_ref[...], ct_ref[...], preferred_element_type=F32)
        stage(vt_ref, vbuf, slot)
        online(s, ma_ref, la_ref, acca_ref,
               lambda pr: lax.dot_general(pr, vt_ref[...], _NT, preferred_element_type=F32))
        online(sc, mc_ref, lc_ref, accc_ref,
               lambda pr: lax.dot_general(pr, ct_ref[0:KV_LORA, :], _NT, preferred_element_type=F32))

    for grp in range(ng):
        slot = grp % 2
        if grp + 1 < ng:
            start_group(seq, grp + 1, 1 - slot)
        else:
            @pl.when(seq + 1 < n_seq)
            def _prefetch_next_seq():
                start_group(seq + 1, 0, 1 - slot)
        past(grp, slot, with_bias=grp == ng - 1)

    pad = page - t_new
    kn = jnp.concatenate([kn_ref[...], jnp.zeros((pad, WA), F32)], axis=0).astype(BF)
    vn = jnp.concatenate([vn_ref[...], jnp.zeros((pad, WV), F32)], axis=0).astype(BF)
    cn = jnp.concatenate([cn_ref[...], jnp.zeros((pad, wc), F32)], axis=0).astype(BF)
    s = lax.dot_general(qs_ref[...], kn, _NT, preferred_element_type=F32) + bn_ref[...]
    online(s, ma_ref, la_ref, acca_ref, lambda pr: jnp.dot(pr, vn, preferred_element_type=F32))
    tok = lax.broadcasted_iota(jnp.int32, (rc, page), 0) % t_new
    col = lax.broadcasted_iota(jnp.int32, (rc, page), 1)
    sc = lax.dot_general(qms_ref[...], cn, _NT, preferred_element_type=F32)
    sc = jnp.where(col <= tok, sc, -jnp.inf)
    online(sc, mc_ref, lc_ref, accc_ref, lambda pr: jnp.dot(pr, cn[:, 0:KV_LORA], preferred_element_type=F32))

    lam = lam_ref[...]
    lane = lax.broadcasted_iota(jnp.int32, (t_new, WV), 1)
    o = acca_ref[...] / la_ref[...]
    out = jnp.zeros((t_new, WV), F32)
    for hh in range(H_A):
        a0 = 2 * hh * t_new
        dh = o[a0:a0 + t_new] - lam * o[a0 + t_new:a0 + 2 * t_new]
        hmask = lane // DV_A == hh
        ms = jnp.sum(jnp.where(hmask, dh * dh, 0.0), axis=-1, keepdims=True) * (1.0 / DV_A)
        out = jnp.where(hmask, dh * lax.rsqrt(ms + NORM_EPS), out)
    oa_ref[...] = out * gain_ref[...]
    ol = (accc_ref[...] / lc_ref[...]).astype(BF)
    oc = jnp.zeros((t_new, H_C * DV_C), F32)
    for hh in range(H_C):
        oc = oc + jnp.dot(ol[hh * t_new:(hh + 1) * t_new], wuv_ref[hh], preferred_element_type=F32)
    oc_ref[...] = oc


def _dec_attn(layer, page_flat, n_pages, qa, qm, kn, vn, cn, bias_last, bias_new, gain, lam, wuv,
              kt_cache, vt_cache, ct_cache, g_pages):
    b, t_new, _ = qa.shape
    page = kt_cache.shape[-1]
    ng = n_pages // g_pages
    assert n_pages % g_pages == 0 and ng % 2 == 0, "two-slot page buffer needs an even number of groups"
    ra, rc = 2 * H_A * t_new, H_C * t_new
    wc = KV_LORA + ROPE_C
    seq = lambda w: pl.BlockSpec((None, t_new, w), lambda bb, pt: (bb, 0, 0))
    const = lambda shape: pl.BlockSpec(shape, lambda bb, pt: (0,) * len(shape))
    hbm = pl.BlockSpec(memory_space=pl.ANY)
    in_specs = [seq(WA), pl.BlockSpec((H_C, None, t_new, wc), lambda bb, pt: (0, bb, 0, 0)),
                seq(WA), seq(WV), seq(wc), const((ra, g_pages * page)), const((ra, page)),
                const((1, WV)), const((1, 1)), const((H_C, KV_LORA, H_C * DV_C)), hbm, hbm, hbm]
    grid_spec = pltpu.PrefetchScalarGridSpec(
        num_scalar_prefetch=1, grid=(b,), in_specs=in_specs,
        out_specs=[seq(WV), seq(H_C * DV_C)],
        scratch_shapes=[
            pltpu.VMEM((2, g_pages, WA, page), F32), pltpu.VMEM((2, g_pages, WV, page), F32),
            pltpu.VMEM((2, g_pages, wc, page), F32), pltpu.SemaphoreType.DMA((3, 2)),
            pltpu.VMEM((ra, WA), BF), pltpu.VMEM((rc, wc), BF),
            pltpu.VMEM((WA, g_pages * page), BF), pltpu.VMEM((WV, g_pages * page), BF),
            pltpu.VMEM((wc, g_pages * page), BF),
            pltpu.VMEM((ra, 1), F32), pltpu.VMEM((ra, 1), F32), pltpu.VMEM((ra, WV), F32),
            pltpu.VMEM((rc, 1), F32), pltpu.VMEM((rc, 1), F32), pltpu.VMEM((rc, KV_LORA), F32),
        ],
    )
    return pl.pallas_call(
        functools.partial(_dec_attn_kernel, layer=layer, n_pages=n_pages, g_pages=g_pages, t_new=t_new),
        grid_spec=grid_spec,
        out_shape=[jax.ShapeDtypeStruct((b, t_new, WV), F32), jax.ShapeDtypeStruct((b, t_new, H_C * DV_C), F32)],
        compiler_params=_cparams(("arbitrary",)), name="dec_attn",
    )(page_flat, qa, qm, kn, vn, cn, bias_last, bias_new, gain, lam, wuv, kt_cache, vt_cache, ct_cache)


def _split_points():
    sizes = [WA, WA, WV, WK_B, WK_B, WV_B, WV_B, GATE_RANK, Q_LORA, KV_LORA, ROPE_C]
    return [int(s) for s in np.cumsum(sizes)]


def _pad_cols(a, w):
    return jnp.pad(a, ((0, 0), (0, w - a.shape[1])))


def _prep_layer(l, p):
    sp = _split_points()
    w_in = p['w_in'][l].astype(BF)
    half = ROPE_C // 2
    swap = np.concatenate([np.arange(half, ROPE_C), np.arange(0, half)])
    kr = w_in[:, sp[9]:sp[10]]
    w_cat = jnp.concatenate([
        w_in[:, :sp[6]], _pad_cols(w_in[:, sp[6]:sp[7]], LANE), _pad_cols(w_in[:, sp[7]:sp[8]], 256),
        w_in[:, sp[8]:sp[9]], jnp.tile(kr, (1, H_C)), jnp.tile(kr[:, swap], (1, H_C))], axis=1)
    assert w_cat.shape[1] == _C_END
    w_uq = p['w_uq'][l].reshape(Q_LORA, H_C, NOPE_C + ROPE_C)
    rope_cols = w_uq[:, :, NOPE_C:]
    wuq = jnp.concatenate([w_uq[:, :, :NOPE_C].reshape(Q_LORA, -1), rope_cols.reshape(Q_LORA, -1),
                           rope_cols[:, :, swap].reshape(Q_LORA, -1)], axis=1)
    wuq = jnp.pad(wuq, ((0, 256 - Q_LORA), (0, 0)))
    eye = jnp.eye(H_C, dtype=F32)
    wuk = jnp.einsum('hcn,hg->hngc', p['w_uk'][l], eye).reshape(H_C * NOPE_C, H_C * KV_LORA)
    wuv = jnp.einsum('hcv,hg->hcgv', p['w_uv'][l], eye).reshape(H_C, KV_LORA, H_C * DV_C)
    lam_init = 0.8 - 0.6 * math.exp(-0.3 * l)
    lq1, lk1, lq2, lk2 = p['diff_lambda'][l].astype(F32)
    lam = jnp.exp(jnp.sum(lq1 * lk1)) - jnp.exp(jnp.sum(lq2 * lk2)) + lam_init
    return {
        'attn_norm': p['attn_norm'][l][None, :], 'w_in': w_cat.astype(BF),
        'w2': jnp.pad(p['w_gla_a2'][l], ((0, LANE - GATE_RANK), (0, 0))).astype(BF),
        'b2': p['b_gla_a'][l][None, :], 'mqn': _pad_cols(p['mla_q_norm'][l][None, :], 256),
        'wuq': wuq.astype(BF), 'wuk': wuk.astype(BF), 'mkn': p['mla_kv_norm'][l][None, :],
        'wuv': wuv.astype(BF), 'wuv_t': jnp.swapaxes(wuv, 1, 2).astype(BF), 'gain': jnp.tile(p['diff_subln'][l] * (1.0 - lam_init), H_A)[None, :],
        'lam': lam.reshape(1, 1), 'gla_norm': p['gla_norm'][l][None, :],
        'w_out': p['w_out'][l].astype(BF), 'ffn_norm': p['ffn_norm'][l][None, :],
        'w_gate': p['w_gate'][l].astype(BF), 'w_up': p['w_up'][l].astype(BF),
        'conv_w': p['conv_w'][l], 'conv_b': p['conv_b'][l][None, :], 'w_down': p['w_down'][l].astype(BF),
    }


def _rope_tables(pos):
    half = ROPE_C // 2
    inv = ROPE_THETA ** (-jnp.arange(half, dtype=F32) / half)
    ang = pos.astype(F32)[:, None] * inv
    cos, sin = jnp.cos(ang), jnp.sin(ang)
    cos_t = jnp.tile(jnp.concatenate([cos, cos], axis=1), (1, LANE // ROPE_C))
    sin_t = jnp.tile(jnp.concatenate([-sin, sin], axis=1), (1, LANE // ROPE_C))
    return cos_t, sin_t


def _t5_bucket(rel):
    n = jnp.maximum(rel, 0)
    max_exact = N_BUCKETS // 2
    nf = jnp.maximum(n, max_exact).astype(F32)
    large = max_exact + (jnp.log(nf / max_exact) / math.log(MAX_DISTANCE / max_exact)
                         * (N_BUCKETS - max_exact)).astype(jnp.int32)
    large = jnp.minimum(large, N_BUCKETS - 1)
    return jnp.where(n < max_exact, n, large)


def _rel_bias_tile(rel_table, rel):
    shifted = (rel_table - rel_table[N_BUCKETS - 1]) * LOG2E
    bias = jnp.moveaxis(shifted[_t5_bucket(rel)].astype(F32), -1, 0)
    return jnp.where(rel >= 0, bias, -jnp.inf)


def _toeplitz_bias_t(rel_table, offset, tq):
    ln = 2 * tq
    m = jnp.arange(ln, dtype=jnp.int32)
    m = jnp.where(m < tq, m, m - ln)
    w = _rel_bias_tile(rel_table, offset + m)
    sheared = jnp.tile(w, (1, tq))[:, :tq * (ln - 1)].reshape(w.shape[0], tq, ln - 1)
    return sheared[:, :, :tq]


def _prompt_trunk(x, p, lws):
    b, t, d = x.shape
    n = b * t
    tm = min(ROW_TILE, t)
    tq = min(ATTN_TILE, t)
    assert tq >= MAX_DISTANCE and t % tq == 0 and t % tm == 0 and t % GLA_CHUNK == 0
    gla_rows = min(GLA_ROWS, t)
    pos = jnp.arange(t, dtype=jnp.int32)
    cos_t, sin_t = _rope_tables(jnp.tile(pos, b))
    bias_t = jnp.stack([_toeplitz_bias_t(p['rel_bias'], 0, tq), _toeplitz_bias_t(p['rel_bias'], tq, tq)])
    s0 = jnp.zeros((b, WV_B, WK_B), F32)
    ones_a = jnp.ones((b, H_A, ONES_ROWS, t), BF)
    ones_c = jnp.ones((b, ONES_ROWS, t), BF)
    ks, vs, cs, ss, gs = [], [], [], [], []
    y = None
    for l, lw in enumerate(lws):
        (qa, kaf, vaf, kab, vab, qg, kg, vg, gg, la, qm, cp, mla) = _inproj(x.reshape(n, d), lw, cos_t, sin_t, tm, BF)
        r3 = lambda a: a.reshape(b, t, a.shape[-1])
        q_t = jnp.swapaxes(r3(qa), 1, 2)
        v_t = jnp.concatenate([jnp.swapaxes(r3(vab), 1, 2).reshape(b, H_A, DV_A, t), ones_a], axis=2)
        oa = _diff_attn(q_t, r3(kab), v_t.reshape(b, H_A * (DV_A + ONES_ROWS), t), bias_t, lw['gain'], lw['lam'], tq)
        c3 = r3(cp)
        c_t = jnp.concatenate([jnp.swapaxes(c3[:, :, 0:KV_LORA], 1, 2), ones_c], axis=1)
        oc = _mla_attn(jnp.swapaxes(qm.reshape(H_C, b, t, 256), 2, 3), c3, c_t, lw['wuv_t'], tq)
        og, st = _gla(r3(qg), r3(kg), r3(vg), r3(la), r3(gg), s0, lw['gla_norm'], gla_rows, GLA_CHUNK, 16, BF)
        last = l == len(lws) - 1
        res = _ffn(x, oa, og, oc, lw, tm, final_w=p['final_norm'][None, :] if last else None)
        x, gt = res[0], res[1]
        if last:
            y = res[2]
        ks.append(kaf.reshape(b, t, H_A, 2 * DK_A))
        vs.append(vaf.reshape(b, t, H_A, DV_A))
        cs.append(mla.reshape(b, t, KV_LORA + ROPE_C))
        ss.append(_state_from_blockdiag(st))
        gs.append(gt[:, -1, 8 - (CONV_W - 1):, :])
    return y, [jnp.stack(a, axis=0) for a in (ks, vs, cs, ss, gs)]


def _state_to_blockdiag(s):
    b = s.shape[0]
    eye = jnp.eye(H_B, dtype=s.dtype)
    return jnp.einsum('bhdv,hg->bhvgd', s, eye).reshape(b, WV_B, WK_B)


def _state_from_blockdiag(st):
    b = st.shape[0]
    s5 = st.reshape(b, H_B, DV_B, H_B, DK_B)
    diag = jnp.stack([s5[:, hh, :, hh, :] for hh in range(H_B)], axis=1)
    return jnp.swapaxes(diag, -1, -2)


def _sample_trunk(x, p, lws, cache_k, cache_v, cache_c, state_gla, state_conv, page_table):
    b, t, d = x.shape
    n = b * t
    depth, n_pool, page = cache_k.shape[0], cache_k.shape[1], cache_k.shape[2]
    n_pages = page_table.shape[1]
    past_len = n_pages * page
    tm = min(ROW_TILE, n)
    g_pages = min(PAGES_PER_STEP, n_pages)
    assert n % tm == 0 and tm % t == 0 and n_pages % g_pages == 0 and t == 8 and page == MAX_DISTANCE
    pos = past_len + jnp.arange(t, dtype=jnp.int32)
    cos_t, sin_t = _rope_tables(jnp.tile(pos, b))
    kt_cache = jnp.transpose(cache_k, (0, 1, 3, 4, 2)).reshape(depth, n_pool, WA, page)
    vt_cache = jnp.transpose(cache_v, (0, 1, 3, 4, 2)).reshape(depth, n_pool, WV, page)
    ct_cache = jnp.transpose(cache_c, (0, 1, 3, 2))
    page_flat = page_table.reshape(-1).astype(jnp.int32)
    tok = jnp.arange(t, dtype=jnp.int32)
    col = jnp.arange(page, dtype=jnp.int32)
    rel_last = page + tok[:, None] - col[None, :]
    rel_new = jnp.where(col[None, :] < t, tok[:, None] - col[None, :], -1)
    rows = lambda bias: jnp.broadcast_to(bias[:, None], (H_A, 2, t, page)).reshape(2 * H_A * t, page)
    bias_last = jnp.pad(rows(_rel_bias_tile(p['rel_bias'], rel_last)), ((0, 0), ((g_pages - 1) * page, 0)))
    bias_new = rows(_rel_bias_tile(p['rel_bias'], rel_new))
    ks, vs, cs, ss, gs = [], [], [], [], []
    y = None
    x = x.reshape(1, n, d)
    for l, lw in enumerate(lws):
        (qa, kaf, vaf, _, _, qg, kg, vg, gg, la, qm, cp, mla) = _inproj(x.reshape(n, d), lw, cos_t, sin_t, tm, F32)
        r3 = lambda a: a.reshape(b, t, a.shape[-1])
        qm4 = qm.reshape(H_C, b, t, 256)
        rope_q = jnp.stack([qm4[hh, :, :, KV_LORA + hh * ROPE_C:KV_LORA + (hh + 1) * ROPE_C] for hh in range(H_C)])
        qmc = jnp.concatenate([qm4[..., 0:KV_LORA], rope_q], axis=-1)
        oa, oc = _dec_attn(l, page_flat, n_pages, r3(qa), qmc, r3(kaf), r3(vaf), r3(mla),
                           bias_last, bias_new, lw['gain'], lw['lam'], lw['wuv'],
                           kt_cache, vt_cache, ct_cache, g_pages)
        og, st = _gla(r3(qg), r3(kg), r3(vg), r3(la), r3(gg), _state_to_blockdiag(state_gla[l]),
                      lw['gla_norm'], t, t, t, F32)
        prev = jnp.repeat(jnp.swapaxes(state_conv[l], 0, 1), t, axis=1)[:, None]
        last = l == len(lws) - 1
        res = _ffn(x, oa.reshape(1, n, -1), og.reshape(1, n, -1), oc.reshape(1, n, -1), lw, tm, prev=prev,
                   final_w=p['final_norm'][None, :] if last else None)
        x, gt = res[0], res[1]
        if last:
            y = res[2]
        ks.append(kaf.reshape(b, t, H_A, 2 * DK_A))
        vs.append(vaf.reshape(b, t, H_A, DV_A))
        cs.append(mla.reshape(b, t, KV_LORA + ROPE_C))
        ss.append(_state_from_blockdiag(st))
        gs.append(gt.reshape(b, t, -1)[:, t - (CONV_W - 1):, :])
    return y.reshape(b, t, d), [jnp.stack(a, axis=0) for a in (ks, vs, cs, ss, gs)]


def kernel(x_prompt, x_sample, cache_diff_k, cache_diff_v, cache_mla, state_gla, state_ffn_conv, page_table,
           attn_norm, w_in, diff_lambda, diff_subln, rel_bias, w_gla_a2, b_gla_a, gla_norm, mla_q_norm, w_uq,
           mla_kv_norm, w_uk, w_uv, w_out, ffn_norm, w_gate, w_up, conv_w, conv_b, w_down, final_norm):
    p = {'attn_norm': attn_norm, 'w_in': w_in, 'diff_lambda': diff_lambda, 'diff_subln': diff_subln,
         'rel_bias': rel_bias, 'w_gla_a2': w_gla_a2, 'b_gla_a': b_gla_a, 'gla_norm': gla_norm,
         'mla_q_norm': mla_q_norm, 'w_uq': w_uq, 'mla_kv_norm': mla_kv_norm, 'w_uk': w_uk, 'w_uv': w_uv,
         'w_out': w_out, 'ffn_norm': ffn_norm, 'w_gate': w_gate, 'w_up': w_up, 'conv_w': conv_w,
         'conv_b': conv_b, 'w_down': w_down, 'final_norm': final_norm}
    lws = [_prep_layer(l, p) for l in range(w_in.shape[0])]
    y_s, new_s = _sample_trunk(x_sample, p, lws, cache_diff_k, cache_diff_v, cache_mla, state_gla,
                               state_ffn_conv, page_table)
    y_p, new_p = _prompt_trunk(x_prompt, p, lws)
    return (y_p, y_s, *new_p, *new_s)
```

```python
import functools
import math

import jax
import jax.numpy as jnp
import numpy as np
from jax import lax
from jax.experimental import pallas as pl
from jax.experimental.pallas import tpu as pltpu

BF = jnp.bfloat16
F32 = jnp.float32

H_A, DK_A, DV_A = 4, 32, 64
H_B, DK_B, DV_B = 4, 64, 128
GATE_RANK, GATE_TAU, GLA_CHUNK = 16, 16.0, 64
H_C, Q_LORA, KV_LORA, NOPE_C, ROPE_C, DV_C = 4, 192, 128, 64, 32, 64
ROPE_THETA = 10000.0
N_BUCKETS, MAX_DISTANCE = 32, 128
NORM_EPS = 1e-6
CONV_W = 3
LOG2E = math.log2(math.e)
ONES_ROWS = 16

WA = H_A * 2 * DK_A
WV = H_A * DV_A
WK_B = H_B * DK_B
WV_B = H_B * DV_B
LANE = 128
VMEM_LIMIT = 56 * 1024 * 1024
ROW_TILE = 256
ATTN_TILE = 512
GLA_ROWS = 512
PAGES_PER_STEP = 32

_C_QA, _C_KA, _C_VA, _C_QG, _C_KG, _C_VG, _C_GG = 0, 256, 512, 768, 1024, 1280, 1792
_C_AG, _C_CQ, _C_CKV, _C_KRA, _C_KRB, _C_END = 2304, 2432, 2688, 2816, 2944, 3072

_NT = (((1,), (1,)), ((), ()))
_TN = (((0,), (0,)), ((), ()))


def _cparams(sem):
    return pltpu.CompilerParams(dimension_semantics=sem, vmem_limit_bytes=VMEM_LIMIT)


def _const_spec(shape):
    nd = len(shape)
    return pl.BlockSpec(shape, lambda *_: (0,) * nd, pipeline_mode=pl.Buffered(1))


def _inproj_kernel(x_ref, nw_ref, w_ref, w2_ref, b2_ref, mqn_ref, wuq_ref, wuk_ref, mkn_ref,
                   cos_ref, sin_ref,
                   qa_ref, kaf_ref, vaf_ref, kab_ref, vab_ref, qg_ref, kg_ref, vg_ref, gg_ref,
                   la_ref, qm_ref, cp_ref, mla_ref):
    x = x_ref[...]
    h = x * lax.rsqrt(jnp.mean(x * x, axis=-1, keepdims=True) + NORM_EPS) * nw_ref[...]
    z = jnp.dot(h.astype(BF), w_ref[...], preferred_element_type=F32)

    qa_ref[...] = (z[:, _C_QA:_C_KA] * (DK_A ** -0.5 * LOG2E)).astype(qa_ref.dtype)
    ka = z[:, _C_KA:_C_VA]
    va = z[:, _C_VA:_C_QG]
    kaf_ref[...] = ka
    vaf_ref[...] = va
    kab_ref[...] = ka.astype(kab_ref.dtype)
    vab_ref[...] = va.astype(vab_ref.dtype)

    qg_ref[...] = z[:, _C_QG:_C_KG] * (DK_B ** -0.5)
    kg_ref[...] = z[:, _C_KG:_C_VG]
    vg_ref[...] = z[:, _C_VG:_C_GG].astype(vg_ref.dtype)
    gg_ref[...] = z[:, _C_GG:_C_AG]
    ag = z[:, _C_AG:_C_CQ]
    xg = jnp.dot(ag.astype(BF), w2_ref[...], preferred_element_type=F32) + b2_ref[...]
    la_ref[...] = (jnp.minimum(xg, 0.0) - jnp.log1p(jnp.exp(-jnp.abs(xg)))) * (1.0 / GATE_TAU)

    cos = cos_ref[...]
    sin = sin_ref[...]
    cq = z[:, _C_CQ:_C_CKV]
    cqn = cq * lax.rsqrt(jnp.sum(cq * cq, axis=-1, keepdims=True) * (1.0 / Q_LORA) + NORM_EPS) * mqn_ref[...]
    qc = jnp.dot(cqn.astype(BF), wuq_ref[...], preferred_element_type=F32)
    scale_c = (NOPE_C + ROPE_C) ** -0.5 * LOG2E
    q_rope = (qc[:, 256:384] * cos + qc[:, 384:512] * sin) * scale_c
    q_lat = jnp.dot(qc[:, 0:256].astype(BF), wuk_ref[...], preferred_element_type=F32) * scale_c
    lane = lax.broadcasted_iota(jnp.int32, q_rope.shape, 1)
    for hh in range(H_C):
        qm_ref[hh, :, 0:KV_LORA] = q_lat[:, hh * KV_LORA:(hh + 1) * KV_LORA].astype(qm_ref.dtype)
        qm_ref[hh, :, KV_LORA:2 * KV_LORA] = jnp.where(lane // ROPE_C == hh, q_rope, 0.0).astype(qm_ref.dtype)
    ckv = z[:, _C_CKV:_C_KRA]
    ckvn = ckv * lax.rsqrt(jnp.mean(ckv * ckv, axis=-1, keepdims=True) + NORM_EPS) * mkn_ref[...]
    krr = z[:, _C_KRA:_C_KRB] * cos + z[:, _C_KRB:_C_END] * sin
    cp_ref[:, 0:KV_LORA] = ckvn.astype(cp_ref.dtype)
    cp_ref[:, KV_LORA:2 * KV_LORA] = krr.astype(cp_ref.dtype)
    mla_ref[:, 0:KV_LORA] = ckvn
    mla_ref[:, KV_LORA:KV_LORA + ROPE_C] = krr[:, 0:ROPE_C]


def _inproj(x, lw, cos_t, sin_t, tm, act_dtype):
    n, d = x.shape
    grid = (n // tm,)
    row = lambda w: pl.BlockSpec((tm, w), lambda i: (i, 0))
    in_specs = [
        row(d), _const_spec((1, d)), _const_spec(lw['w_in'].shape), _const_spec(lw['w2'].shape),
        _const_spec((1, WK_B)), _const_spec((1, 256)), _const_spec(lw['wuq'].shape),
        _const_spec(lw['wuk'].shape), _const_spec((1, KV_LORA)), row(LANE), row(LANE),
    ]
    outs = [
        (WA, act_dtype), (WA, F32), (WV, F32), (WA, act_dtype), (WV, act_dtype),
        (WK_B, F32), (WK_B, F32), (WV_B, act_dtype), (WV_B, F32), (WK_B, F32),
    ]
    out_shape = [jax.ShapeDtypeStruct((n, w), dt) for w, dt in outs]
    out_specs = [row(w) for w, _ in outs]
    out_shape += [jax.ShapeDtypeStruct((H_C, n, 256), act_dtype), jax.ShapeDtypeStruct((n, 256), act_dtype),
                  jax.ShapeDtypeStruct((n, KV_LORA + ROPE_C), F32)]
    out_specs += [pl.BlockSpec((H_C, tm, 256), lambda i: (0, i, 0)), row(256),
                  pl.BlockSpec((tm, KV_LORA + ROPE_C), lambda i: (i, 0))]
    return pl.pallas_call(
        _inproj_kernel, grid=grid, in_specs=in_specs, out_specs=out_specs, out_shape=out_shape,
        compiler_params=_cparams(("parallel",)), name="inproj",
    )(x, lw['attn_norm'], lw['w_in'], lw['w2'], lw['b2'], lw['mqn'], lw['wuq'], lw['wuk'], lw['mkn'],
      cos_t, sin_t)


def _diff_attn_kernel(qi_ref, ki_ref, qT_ref, k_ref, vT_ref, bias_ref, gain_ref, lam_ref, o_ref,
                      qs_ref, m_ref, acc_ref, *, tq):
    p = pl.program_id(1)
    qi = qi_ref[p]
    ki = ki_ref[p]
    nmap = 2 * H_A
    va = DV_A + ONES_ROWS

    @pl.when(ki == 0)
    def _init():
        qT = qT_ref[...]
        row = lax.broadcasted_iota(jnp.int32, qT.shape, 0)
        zero = jnp.zeros_like(qT)
        for hm in range(nmap):
            qs_ref[:, hm * tq:(hm + 1) * tq] = jnp.where(row // DK_A == hm, qT, zero)
        m_ref[...] = jnp.full(m_ref.shape, -jnp.inf, F32)
        acc_ref[...] = jnp.zeros(acc_ref.shape, F32)

    def step(with_bias):
        k = k_ref[...]

        def qk(hm):
            s = jnp.dot(k, qs_ref[:, hm * tq:(hm + 1) * tq], preferred_element_type=F32)
            return s + bias_ref[qi - ki, hm // 2] if with_bias else s

        s_next = qk(0)
        for hm in range(nmap):
            hh = hm // 2
            s = s_next
            if hm + 1 < nmap:
                s_next = qk(hm + 1)
            m_old = m_ref[hm:hm + 1, :]
            m_new = jnp.maximum(m_old, jnp.max(s, axis=0, keepdims=True))
            alpha = jnp.exp2(m_old - m_new)
            pr = jnp.exp2(s - m_new).astype(BF)
            pv = jnp.dot(vT_ref[hh * va:(hh + 1) * va, :], pr, preferred_element_type=F32)
            acc_ref[hm] = alpha * acc_ref[hm] + pv
            m_ref[hm:hm + 1, :] = m_new

    pl.when(qi - ki <= 1)(lambda: step(True))
    pl.when(qi - ki > 1)(lambda: step(False))

    @pl.when(ki == qi)
    def _fin():
        lam = lam_ref[...]
        outs = []
        for hh in range(H_A):
            a1 = acc_ref[2 * hh]
            a2 = acc_ref[2 * hh + 1]
            dh = a1[0:DV_A] / a1[DV_A:DV_A + 1] - lam * (a2[0:DV_A] / a2[DV_A:DV_A + 1])
            ms = jnp.mean(dh * dh, axis=0, keepdims=True)
            outs.append(dh * lax.rsqrt(ms + NORM_EPS))
        out_t = jnp.concatenate(outs, axis=0)
        o_ref[...] = (out_t.T * gain_ref[...]).astype(o_ref.dtype)


def _causal_pairs(nq):
    qi = np.concatenate([np.full(i + 1, i, np.int32) for i in range(nq)])
    ki = np.concatenate([np.arange(i + 1, dtype=np.int32) for i in range(nq)])
    return jnp.asarray(qi), jnp.asarray(ki)


def _diff_attn(q_t, k, v_t, bias_t, gain, lam, tq):
    b, _, t = q_t.shape
    nq = t // tq
    va = DV_A + ONES_ROWS
    qi, ki = _causal_pairs(nq)
    grid_spec = pltpu.PrefetchScalarGridSpec(
        num_scalar_prefetch=2, grid=(b, int(qi.shape[0])),
        in_specs=[
            pl.BlockSpec((None, WA, tq), lambda bb, p, qi, ki: (bb, 0, qi[p])),
            pl.BlockSpec((None, tq, WA), lambda bb, p, qi, ki: (bb, ki[p], 0)),
            pl.BlockSpec((None, H_A * va, tq), lambda bb, p, qi, ki: (bb, 0, ki[p])),
            pl.BlockSpec((2, H_A, tq, tq), lambda bb, p, qi, ki: (0, 0, 0, 0), pipeline_mode=pl.Buffered(1)),
            pl.BlockSpec((1, WV), lambda bb, p, qi, ki: (0, 0)),
            pl.BlockSpec((1, 1), lambda bb, p, qi, ki: (0, 0)),
        ],
        out_specs=pl.BlockSpec((None, tq, WV), lambda bb, p, qi, ki: (bb, qi[p], 0)),
        scratch_shapes=[
            pltpu.VMEM((WA, 2 * H_A * tq), BF), pltpu.VMEM((2 * H_A, tq), F32),
            pltpu.VMEM((2 * H_A, va, tq), F32),
        ],
    )
    return pl.pallas_call(
        functools.partial(_diff_attn_kernel, tq=tq), grid_spec=grid_spec,
        out_shape=jax.ShapeDtypeStruct((b, t, WV), BF),
        compiler_params=_cparams(("parallel", "arbitrary")), name="diff_attn",
    )(qi, ki, q_t, k, v_t, bias_t, gain, lam)


def _mla_attn_kernel(qi_ref, ki_ref, qmT_ref, c_ref, cT_ref, wuvT_ref, o_ref, m_ref, acc_ref, *, tq):
    p = pl.program_id(1)
    qi = qi_ref[p]
    ki = ki_ref[p]

    @pl.when(ki == 0)
    def _init():
        m_ref[...] = jnp.full(m_ref.shape, -jnp.inf, F32)
        acc_ref[...] = jnp.zeros(acc_ref.shape, F32)

    def step(masked):
        c = c_ref[...]
        if masked:
            causal = (lax.broadcasted_iota(jnp.int32, (tq, tq), 0) <= lax.broadcasted_iota(jnp.int32, (tq, tq), 1))

        def qk(hh):
            s = jnp.dot(c, qmT_ref[hh], preferred_element_type=F32)
            return jnp.where(causal, s, -jnp.inf) if masked else s

        s_next = qk(0)
        for hh in range(H_C):
            s = s_next
            if hh + 1 < H_C:
                s_next = qk(hh + 1)
            m_old = m_ref[hh:hh + 1, :]
            m_new = jnp.maximum(m_old, jnp.max(s, axis=0, keepdims=True))
            alpha = jnp.exp2(m_old - m_new)
            pr = jnp.exp2(s - m_new).astype(BF)
            acc_ref[hh] = alpha * acc_ref[hh] + jnp.dot(cT_ref[...], pr, preferred_element_type=F32)
            m_ref[hh:hh + 1, :] = m_new

    pl.when(ki == qi)(lambda: step(True))
    pl.when(ki != qi)(lambda: step(False))

    @pl.when(ki == qi)
    def _fin():
        out_t = jnp.zeros((H_C * DV_C, tq), F32)
        for hh in range(H_C):
            a = acc_ref[hh]
            o_lat = (a[0:KV_LORA] / a[KV_LORA:KV_LORA + 1]).astype(BF)
            out_t = out_t + jnp.dot(wuvT_ref[hh], o_lat, preferred_element_type=F32)
        o_ref[...] = out_t.T.astype(o_ref.dtype)


def _mla_attn(qm_t, c, c_t, wuv_t, tq):
    b, t, _ = c.shape
    nq = t // tq
    qi, ki = _causal_pairs(nq)
    rows = KV_LORA + ONES_ROWS
    grid_spec = pltpu.PrefetchScalarGridSpec(
        num_scalar_prefetch=2, grid=(b, int(qi.shape[0])),
        in_specs=[
            pl.BlockSpec((H_C, None, 256, tq), lambda bb, p, qi, ki: (0, bb, 0, qi[p])),
            pl.BlockSpec((None, tq, 256), lambda bb, p, qi, ki: (bb, ki[p], 0)),
            pl.BlockSpec((None, rows, tq), lambda bb, p, qi, ki: (bb, 0, ki[p])),
            pl.BlockSpec((H_C, H_C * DV_C, KV_LORA), lambda bb, p, qi, ki: (0, 0, 0)),
        ],
        out_specs=pl.BlockSpec((None, tq, H_C * DV_C), lambda bb, p, qi, ki: (bb, qi[p], 0)),
        scratch_shapes=[pltpu.VMEM((H_C, tq), F32), pltpu.VMEM((H_C, rows, tq), F32)],
    )
    return pl.pallas_call(
        functools.partial(_mla_attn_kernel, tq=tq), grid_spec=grid_spec,
        out_shape=jax.ShapeDtypeStruct((b, t, H_C * DV_C), BF),
        compiler_params=_cparams(("parallel", "arbitrary")), name="mla_attn",
    )(qi, ki, qm_t, c, c_t, wuv_t)


def _gla_kernel(q_ref, k_ref, v_ref, g_ref, gate_ref, s0_ref, nw_ref, o_ref, so_ref, st_ref,
                *, chunk, sub, nchunk):
    t = pl.program_id(1)

    @pl.when(t == 0)
    def _init():
        st_ref[...] = s0_ref[...]

    C, c = chunk, sub
    tri = (lax.broadcasted_iota(jnp.int32, (C, C), 0) >= lax.broadcasted_iota(jnp.int32, (C, C), 1)).astype(F32)
    ind = (lax.broadcasted_iota(jnp.int32, (WK_B, WV_B), 0) // DK_B
           == lax.broadcasted_iota(jnp.int32, (WK_B, WV_B), 1) // DV_B).astype(BF)
    bdmask = (lax.broadcasted_iota(jnp.int32, (WV_B, WK_B), 0) // DV_B
              == lax.broadcasted_iota(jnp.int32, (WV_B, WK_B), 1) // DK_B)
    lane_k = lax.broadcasted_iota(jnp.int32, (c, WK_B), 1)
    row_c = lax.broadcasted_iota(jnp.int32, (c, WK_B), 0)
    nw = nw_ref[...]

    def do_chunk(n, carry):
        r = pl.multiple_of(n * C, C)
        q = q_ref[pl.ds(r, C), :]
        k = k_ref[pl.ds(r, C), :]
        v = v_ref[pl.ds(r, C), :].astype(BF)
        g = g_ref[pl.ds(r, C), :]
        b = jnp.dot(tri, g, preferred_element_type=F32, precision=lax.Precision.HIGHEST)
        b_last = b[C - 1:C, :]
        st = st_ref[...]
        qt = (q * jnp.exp(b)).astype(BF)
        o = lax.dot_general(qt, st.astype(BF), _NT, preferred_element_type=F32)
        kt = (k * jnp.exp(b_last - b)).astype(BF)
        u = lax.dot_general(v, kt, _TN, preferred_element_type=F32)
        st_ref[...] = st * jnp.exp(b_last) + jnp.where(bdmask, u, 0.0)
        rows = []
        for blk in range(C // c):
            lo, hi = blk * c, (blk + 1) * c
            b_i, q_i, k_i = b[lo:hi], q[lo:hi], k[lo:hi]
            v_i = v[lo:hi].astype(F32)
            es = []
            for j in range(c):
                e = jnp.exp(jnp.minimum(b_i - b_i[j:j + 1], 0.0)) * q_i * k_i[j:j + 1]
                es.append(jnp.where(row_c >= j, e, 0.0))
            rr = jnp.dot(jnp.concatenate(es, axis=0).astype(BF), ind, preferred_element_type=F32)
            od = rr[0:c] * v_i[0:1]
            for j in range(1, c):
                od = od + rr[j * c:(j + 1) * c] * v_i[j:j + 1]
            if blk > 0:
                b_s = b[lo - 1:lo]
                qh = q_i * jnp.exp(b_i - b_s)
                kh = (k[0:lo] * jnp.exp(b_s - b[0:lo])).astype(BF)
                qs = jnp.concatenate([jnp.where(lane_k // DK_B == hh, qh, 0.0) for hh in range(H_B)], axis=0)
                a = lax.dot_general(qs.astype(BF), kh, _NT, preferred_element_type=F32)
                pv = jnp.dot(a.astype(BF), v[0:lo], preferred_element_type=F32)
                od = od + jnp.concatenate(
                    [pv[hh * c:(hh + 1) * c, hh * DV_B:(hh + 1) * DV_B] for hh in range(H_B)], axis=1)
            rows.append(od)
        o = o + (jnp.concatenate(rows, axis=0) if len(rows) > 1 else rows[0])
        gate = gate_ref[pl.ds(r, C), :]
        outs = []
        for hh in range(H_B):
            oh = o[:, hh * DV_B:(hh + 1) * DV_B]
            outs.append(oh * lax.rsqrt(jnp.mean(oh * oh, axis=-1, keepdims=True) + NORM_EPS) * nw)
        y = jnp.concatenate(outs, axis=1) * (gate * jax.nn.sigmoid(gate))
        o_ref[pl.ds(r, C), :] = y.astype(o_ref.dtype)
        return carry

    if nchunk == 1:
        do_chunk(0, 0)
    else:
        lax.fori_loop(0, nchunk, do_chunk, 0, unroll=2)

    @pl.when(t == pl.num_programs(1) - 1)
    def _fin():
        so_ref[...] = st_ref[...]


def _gla(q, k, v, g, gate, s0t, nw, rows, chunk, sub, out_dtype):
    b, t, _ = q.shape
    nt = t // rows
    blk = lambda w: pl.BlockSpec((None, rows, w), lambda bb, tt: (bb, tt, 0))
    st_spec = pl.BlockSpec((None, WV_B, WK_B), lambda bb, tt: (bb, 0, 0))
    return pl.pallas_call(
        functools.partial(_gla_kernel, chunk=chunk, sub=sub, nchunk=rows // chunk),
        grid=(b, nt),
        in_specs=[blk(WK_B), blk(WK_B), blk(WV_B), blk(WK_B), blk(WV_B), st_spec,
                  pl.BlockSpec((1, DV_B), lambda bb, tt: (0, 0))],
        out_specs=[blk(WV_B), st_spec],
        out_shape=[jax.ShapeDtypeStruct((b, t, WV_B), out_dtype), jax.ShapeDtypeStruct((b, WV_B, WK_B), F32)],
        scratch_shapes=[pltpu.VMEM((WV_B, WK_B), F32)],
        compiler_params=_cparams(("parallel", "arbitrary")), name="gla",
    )(q, k, v, g, gate, s0t, nw)


def _ffn_kernel(*refs, tm, per_seq_rows, with_final):
    it = iter(refs)
    x_ref, oa_ref, og_ref, oc_ref, wo_ref, fnw_ref, wg_ref, wu_ref, cw_ref, cb_ref, wd_ref = (next(it) for _ in range(11))
    prev_ref = next(it) if per_seq_rows else None
    finw_ref = next(it) if with_final else None
    xo_ref, gt_ref = next(it), next(it)
    y_ref = next(it) if with_final else None
    carry_ref = None if per_seq_rows else next(it)

    x = x_ref[...]
    mix = (jnp.dot(oa_ref[...].astype(BF), wo_ref[0:WV, :], preferred_element_type=F32)
           + jnp.dot(og_ref[...].astype(BF), wo_ref[WV:WV + WV_B, :], preferred_element_type=F32)
           + jnp.dot(oc_ref[...].astype(BF), wo_ref[WV + WV_B:, :], preferred_element_type=F32))
    xm = x + mix
    hb = (xm * lax.rsqrt(jnp.mean(xm * xm, axis=-1, keepdims=True) + NORM_EPS) * fnw_ref[...]).astype(BF)
    g = jnp.dot(hb, wg_ref[...], preferred_element_type=F32)
    u = jnp.dot(hb, wu_ref[...], preferred_element_type=F32)
    row = lax.broadcasted_iota(jnp.int32, g.shape, 0)
    g1 = pltpu.roll(g, 1, axis=0)
    g2 = pltpu.roll(g, 2, axis=0)
    if per_seq_rows:
        rm = row % per_seq_rows
        p0, p1 = prev_ref[0], prev_ref[1]
        g1 = jnp.where(rm == 0, p1, g1)
        g2 = jnp.where(rm == 0, p0, jnp.where(rm == 1, p1, g2))
        gt_ref[...] = g
    else:
        @pl.when(pl.program_id(1) == 0)
        def _zero():
            carry_ref[...] = jnp.zeros(carry_ref.shape, F32)
        p0, p1 = carry_ref[6:7, :], carry_ref[7:8, :]
        g1 = jnp.where(row == 0, p1, g1)
        g2 = jnp.where(row == 0, p0, jnp.where(row == 1, p1, g2))
        tail = g[tm - 8:tm, :]
        carry_ref[...] = tail
        gt_ref[...] = tail
    gc = cb_ref[...] + g2 * cw_ref[0:1, :] + g1 * cw_ref[1:2, :] + g * cw_ref[2:3, :]
    act = (gc * jax.nn.sigmoid(gc) * u).astype(BF)
    xo = xm + jnp.dot(act, wd_ref[...], preferred_element_type=F32)
    xo_ref[...] = xo
    if with_final:
        y_ref[...] = xo * lax.rsqrt(jnp.mean(xo * xo, axis=-1, keepdims=True) + NORM_EPS) * finw_ref[...]


def _ffn(x, oa, og, oc, lw, tm, prev=None, final_w=None):
    b, t, d = x.shape
    f = lw['w_gate'].shape[1]
    nt = t // tm
    per_seq = prev is not None
    blk = lambda w: pl.BlockSpec((None, tm, w), lambda bb, tt: (bb, tt, 0))
    in_specs = [blk(d), blk(WV), blk(WV_B), blk(H_C * DV_C), _const_spec(lw['w_out'].shape), _const_spec((1, d)),
                _const_spec(lw['w_gate'].shape), _const_spec(lw['w_up'].shape), _const_spec((CONV_W, f)),
                _const_spec((1, f)), _const_spec(lw['w_down'].shape)]
    args = [x, oa, og, oc, lw['w_out'], lw['ffn_norm'], lw['w_gate'], lw['w_up'], lw['conv_w'], lw['conv_b'],
            lw['w_down']]
    if per_seq:
        in_specs.append(pl.BlockSpec((2, None, tm, f), lambda bb, tt: (0, bb, tt, 0)))
        args.append(prev)
    if final_w is not None:
        in_specs.append(_const_spec((1, d)))
        args.append(final_w)
    gr = tm if per_seq else 8
    out_shape = [jax.ShapeDtypeStruct((b, t, d), F32), jax.ShapeDtypeStruct((b, nt, gr, f), F32)]
    out_specs = [blk(d), pl.BlockSpec((None, None, gr, f), lambda bb, tt: (bb, tt, 0, 0))]
    if final_w is not None:
        out_shape.append(jax.ShapeDtypeStruct((b, t, d), F32))
        out_specs.append(blk(d))
    scratch = [] if per_seq else [pltpu.VMEM((8, f), F32)]
    return pl.pallas_call(
        functools.partial(_ffn_kernel, tm=tm, per_seq_rows=8 if per_seq else 0, with_final=final_w is not None),
        grid=(b, nt), in_specs=in_specs, out_specs=out_specs, out_shape=out_shape, scratch_shapes=scratch,
        compiler_params=_cparams(("parallel", "arbitrary")), name="ffn",
    )(*args)


def _dec_attn_kernel(pt_ref, qa_ref, qm_ref, kn_ref, vn_ref, cn_ref, bl_ref, bn_ref, gain_ref, lam_ref, wuv_ref,
                     kt_hbm, vt_hbm, ct_hbm, oa_ref, oc_ref,
                     kbuf, vbuf, cbuf, sem, qs_ref, qms_ref, kt_ref, vt_ref, ct_ref,
                     ma_ref, la_ref, acca_ref, mc_ref, lc_ref, accc_ref,
                     *, layer, n_pages, g_pages, t_new):
    seq = pl.program_id(0)
    n_seq = pl.num_programs(0)
    ng = n_pages // g_pages
    rc = H_C * t_new
    page = kbuf.shape[-1]
    wc = KV_LORA + ROPE_C

    streams = ((kt_hbm, kbuf), (ct_hbm, cbuf), (vt_hbm, vbuf))

    def stream_copies(kind, sq, grp, slot):
        hbm, buf = streams[kind]
        return [pltpu.make_async_copy(hbm.at[layer, pt_ref[sq * n_pages + grp * g_pages + i]], buf.at[slot, i],
                                      sem.at[kind, slot]) for i in range(g_pages)]

    def start_group(sq, grp, slot):
        for kind in range(len(streams)):
            for i, cp in enumerate(stream_copies(kind, sq, grp, slot)):
                cp.start(priority=i % 2)

    def wait_stream(kind, sq, grp, slot):
        for cp in stream_copies(kind, sq, grp, slot):
            cp.wait()

    @pl.when(seq == 0)
    def _first():
        start_group(0, 0, 0)

    q = qa_ref[...]
    lane_q = lax.broadcasted_iota(jnp.int32, q.shape, 1)
    qs = jnp.concatenate([jnp.where(lane_q // DK_A == hm, q, 0.0) for hm in range(2 * H_A)], axis=0)
    qs_ref[...] = qs.astype(BF)
    qms_ref[...] = qm_ref[...].reshape(rc, wc).astype(BF)
    ma_ref[...] = jnp.full(ma_ref.shape, -jnp.inf, F32)
    la_ref[...] = jnp.zeros(la_ref.shape, F32)
    acca_ref[...] = jnp.zeros(acca_ref.shape, F32)
    mc_ref[...] = jnp.full(mc_ref.shape, -jnp.inf, F32)
    lc_ref[...] = jnp.zeros(lc_ref.shape, F32)
    accc_ref[...] = jnp.zeros(accc_ref.shape, F32)

    def online(s, m_ref, l_ref, acc_ref, pv):
        m_old = m_ref[...]
        m_new = jnp.maximum(m_old, jnp.max(s, axis=-1, keepdims=True))
        alpha = jnp.exp2(m_old - m_new)
        pr = jnp.exp2(s - m_new)
        l_ref[...] = alpha * l_ref[...] + jnp.sum(pr, axis=-1, keepdims=True)
        acc_ref[...] = alpha * acc_ref[...] + pv(pr.astype(BF))
        m_ref[...] = m_new

    def stage(dst_ref, buf, slot):
        for gg in range(g_pages):
            dst_ref[:, gg * page:(gg + 1) * page] = buf[slot, gg].astype(BF)

    def past(grp, slot, with_bias):
        for kind in range(len(streams)):
            wait_stream(kind, seq, grp, slot)
        stage(kt_ref, kbuf, slot)
        s = jnp.dot(qs_ref[...], kt_ref[...], preferred_element_type=F32)
        if with_bias:
            s = s + bl_ref[...]
        stage(ct_ref, cbuf, slot)
        sc = jnp.dot(qms_ref[...], ct_ref[...], preferred_element_type=F32)
        stage(vt_ref, vbuf, slot)
        online(s, ma_ref, la_ref, acca_ref,
               lambda pr: lax.dot_general(pr, vt_ref[...], _NT, preferred_element_type=F32))
        online(sc, mc_ref, lc_ref, accc_ref,
               lambda pr: lax.dot_general(pr, ct_ref[0:KV_LORA, :], _NT, preferred_element_type=F32))

    for grp in range(ng):
        slot = grp % 2
        if grp + 1 < ng:
            start_group(seq, grp + 1, 1 - slot)
        else:
            @pl.when(seq + 1 < n_seq)
            def _prefetch_next_seq():
                start_group(seq + 1, 0, 1 - slot)
        past(grp, slot, with_bias=grp == ng - 1)

    pad = page - t_new
    kn = jnp.concatenate([kn_ref[...], jnp.zeros((pad, WA), F32)], axis=0).astype(BF)
    vn = jnp.concatenate([vn_ref[...], jnp.zeros((pad, WV), F32)], axis=0).astype(BF)
    cn = jnp.concatenate([cn_ref[...], jnp.zeros((pad, wc), F32)], axis=0).astype(BF)
    s = lax.dot_general(qs_ref[...], kn, _NT, preferred_element_type=F32) + bn_ref[...]
    online(s, ma_ref, la_ref, acca_ref, lambda pr: jnp.dot(pr, vn, preferred_element_type=F32))
    tok = lax.broadcasted_iota(jnp.int32, (rc, page), 0) % t_new
    col = lax.broadcasted_iota(jnp.int32, (rc, page), 1)
    sc = lax.dot_general(qms_ref[...], cn, _NT, preferred_element_type=F32)
    sc = jnp.where(col <= tok, sc, -jnp.inf)
    online(sc, mc_ref, lc_ref, accc_ref, lambda pr: jnp.dot(pr, cn[:, 0:KV_LORA], preferred_element_type=F32))

    lam = lam_ref[...]
    lane = lax.broadcasted_iota(jnp.int32, (t_new, WV), 1)
    o = acca_ref[...] / la_ref[...]
    out = jnp.zeros((t_new, WV), F32)
    for hh in range(H_A):
        a0 = 2 * hh * t_new
        dh = o[a0:a0 + t_new] - lam * o[a0 + t_new:a0 + 2 * t_new]
        hmask = lane // DV_A == hh
        ms = jnp.sum(jnp.where(hmask, dh * dh, 0.0), axis=-1, keepdims=True) * (1.0 / DV_A)
        out = jnp.where(hmask, dh * lax.rsqrt(ms + NORM_EPS), out)
    oa_ref[...] = out * gain_ref[...]
    ol = (accc_ref[...] / lc_ref[...]).astype(BF)
    oc = jnp.zeros((t_new, H_C * DV_C), F32)
    for hh in range(H_C):
        oc = oc + jnp.dot(ol[hh * t_new:(hh + 1) * t_new], wuv_ref[hh], preferred_element_type=F32)
    oc_ref[...] = oc


def _dec_attn(layer, page_flat, n_pages, qa, qm, kn, vn, cn, bias_last, bias_new, gain, lam, wuv,
              kt_cache, vt_cache, ct_cache, g_pages):
    b, t_new, _ = qa.shape
    page = kt_cache.shape[-1]
    ng = n_pages // g_pages
    assert n_pages % g_pages == 0 and ng % 2 == 0, "two-slot page buffer needs an even number of groups"
    ra, rc = 2 * H_A * t_new, H_C * t_new
    wc = KV_LORA + ROPE_C
    seq = lambda w: pl.BlockSpec((None, t_new, w), lambda bb, pt: (bb, 0, 0))
    const = lambda shape: pl.BlockSpec(shape, lambda bb, pt: (0,) * len(shape))
    hbm = pl.BlockSpec(memory_space=pl.ANY)
    in_specs = [seq(WA), pl.BlockSpec((H_C, None, t_new, wc), lambda bb, pt: (0, bb, 0, 0)),
                seq(WA), seq(WV), seq(wc), const((ra, g_pages * page)), const((ra, page)),
                const((1, WV)), const((1, 1)), const((H_C, KV_LORA, H_C * DV_C)), hbm, hbm, hbm]
    grid_spec = pltpu.PrefetchScalarGridSpec(
        num_scalar_prefetch=1, grid=(b,), in_specs=in_specs,
        out_specs=[seq(WV), seq(H_C * DV_C)],
        scratch_shapes=[
            pltpu.VMEM((2, g_pages, WA, page), F32), pltpu.VMEM((2, g_pages, WV, page), F32),
            pltpu.VMEM((2, g_pages, wc, page), F32), pltpu.SemaphoreType.DMA((3, 2)),
            pltpu.VMEM((ra, WA), BF), pltpu.VMEM((rc, wc), BF),
            pltpu.VMEM((WA, g_pages * page), BF), pltpu.VMEM((WV, g_pages * page), BF),
            pltpu.VMEM((wc, g_pages * page), BF),
            pltpu.VMEM((ra, 1), F32), pltpu.VMEM((ra, 1), F32), pltpu.VMEM((ra, WV), F32),
            pltpu.VMEM((rc, 1), F32), pltpu.VMEM((rc, 1), F32), pltpu.VMEM((rc, KV_LORA), F32),
        ],
    )
    return pl.pallas_call(
        functools.partial(_dec_attn_kernel, layer=layer, n_pages=n_pages, g_pages=g_pages, t_new=t_new),
        grid_spec=grid_spec,
        out_shape=[jax.ShapeDtypeStruct((b, t_new, WV), F32), jax.ShapeDtypeStruct((b, t_new, H_C * DV_C), F32)],
        compiler_params=_cparams(("arbitrary",)), name="dec_attn",
    )(page_flat, qa, qm, kn, vn, cn, bias_last, bias_new, gain, lam, wuv, kt_cache, vt_cache, ct_cache)


def _split_points():
    sizes = [WA, WA, WV, WK_B, WK_B, WV_B, WV_B, GATE_RANK, Q_LORA, KV_LORA, ROPE_C]
    return [int(s) for s in np.cumsum(sizes)]


def _pad_cols(a, w):
    return jnp.pad(a, ((0, 0), (0, w - a.shape[1])))


def _prep_layer(l, p):
    sp = _split_points()
    w_in = p['w_in'][l].astype(BF)
    half = ROPE_C // 2
    swap = np.concatenate([np.arange(half, ROPE_C), np.arange(0, half)])
    kr = w_in[:, sp[9]:sp[10]]
    w_cat = jnp.concatenate([
        w_in[:, :sp[6]], _pad_cols(w_in[:, sp[6]:sp[7]], LANE), _pad_cols(w_in[:, sp[7]:sp[8]], 256),
        w_in[:, sp[8]:sp[9]], jnp.tile(kr, (1, H_C)), jnp.tile(kr[:, swap], (1, H_C))], axis=1)
    assert w_cat.shape[1] == _C_END
    w_uq = p['w_uq'][l].reshape(Q_LORA, H_C, NOPE_C + ROPE_C)
    rope_cols = w_uq[:, :, NOPE_C:]
    wuq = jnp.concatenate([w_uq[:, :, :NOPE_C].reshape(Q_LORA, -1), rope_cols.reshape(Q_LORA, -1),
                           rope_cols[:, :, swap].reshape(Q_LORA, -1)], axis=1)
    wuq = jnp.pad(wuq, ((0, 256 - Q_LORA), (0, 0)))
    eye = jnp.eye(H_C, dtype=F32)
    wuk = jnp.einsum('hcn,hg->hngc', p['w_uk'][l], eye).reshape(H_C * NOPE_C, H_C * KV_LORA)
    wuv = jnp.einsum('hcv,hg->hcgv', p['w_uv'][l], eye).reshape(H_C, KV_LORA, H_C * DV_C)
    lam_init = 0.8 - 0.6 * math.exp(-0.3 * l)
    lq1, lk1, lq2, lk2 = p['diff_lambda'][l].astype(F32)
    lam = jnp.exp(jnp.sum(lq1 * lk1)) - jnp.exp(jnp.sum(lq2 * lk2)) + lam_init
    return {
        'attn_norm': p['attn_norm'][l][None, :], 'w_in': w_cat.astype(BF),
        'w2': jnp.pad(p['w_gla_a2'][l], ((0, LANE - GATE_RANK), (0, 0))).astype(BF),
        'b2': p['b_gla_a'][l][None, :], 'mqn': _pad_cols(p['mla_q_norm'][l][None, :], 256),
        'wuq': wuq.astype(BF), 'wuk': wuk.astype(BF), 'mkn': p['mla_kv_norm'][l][None, :],
        'wuv': wuv.astype(BF), 'wuv_t': jnp.swapaxes(wuv, 1, 2).astype(BF), 'gain': jnp.tile(p['diff_subln'][l] * (1.0 - lam_init), H_A)[None, :],
        'lam': lam.reshape(1, 1), 'gla_norm': p['gla_norm'][l][None, :],
        'w_out': p['w_out'][l].astype(BF), 'ffn_norm': p['ffn_norm'][l][None, :],
        'w_gate': p['w_gate'][l].astype(BF), 'w_up': p['w_up'][l].astype(BF),
        'conv_w': p['conv_w'][l], 'conv_b': p['conv_b'][l][None, :], 'w_down': p['w_down'][l].astype(BF),
    }


def _rope_tables(pos):
    half = ROPE_C // 2
    inv = ROPE_THETA ** (-jnp.arange(half, dtype=F32) / half)
    ang = pos.astype(F32)[:, None] * inv
    cos, sin = jnp.cos(ang), jnp.sin(ang)
    cos_t = jnp.tile(jnp.concatenate([cos, cos], axis=1), (1, LANE // ROPE_C))
    sin_t = jnp.tile(jnp.concatenate([-sin, sin], axis=1), (1, LANE // ROPE_C))
    return cos_t, sin_t


def _t5_bucket(rel):
    n = jnp.maximum(rel, 0)
    max_exact = N_BUCKETS // 2
    nf = jnp.maximum(n, max_exact).astype(F32)
    large = max_exact + (jnp.log(nf / max_exact) / math.log(MAX_DISTANCE / max_exact)
                         * (N_BUCKETS - max_exact)).astype(jnp.int32)
    large = jnp.minimum(large, N_BUCKETS - 1)
    return jnp.where(n < max_exact, n, large)


def _rel_bias_tile(rel_table, rel):
    shifted = (rel_table - rel_table[N_BUCKETS - 1]) * LOG2E
    bias = jnp.moveaxis(shifted[_t5_bucket(rel)].astype(F32), -1, 0)
    return jnp.where(rel >= 0, bias, -jnp.inf)


def _toeplitz_bias_t(rel_table, offset, tq):
    ln = 2 * tq
    m = jnp.arange(ln, dtype=jnp.int32)
    m = jnp.where(m < tq, m, m - ln)
    w = _rel_bias_tile(rel_table, offset + m)
    sheared = jnp.tile(w, (1, tq))[:, :tq * (ln - 1)].reshape(w.shape[0], tq, ln - 1)
    return sheared[:, :, :tq]


def _prompt_trunk(x, p, lws):
    b, t, d = x.shape
    n = b * t
    tm = min(ROW_TILE, t)
    tq = min(ATTN_TILE, t)
    assert tq >= MAX_DISTANCE and t % tq == 0 and t % tm == 0 and t % GLA_CHUNK == 0
    gla_rows = min(GLA_ROWS, t)
    pos = jnp.arange(t, dtype=jnp.int32)
    cos_t, sin_t = _rope_tables(jnp.tile(pos, b))
    bias_t = jnp.stack([_toeplitz_bias_t(p['rel_bias'], 0, tq), _toeplitz_bias_t(p['rel_bias'], tq, tq)])
    s0 = jnp.zeros((b, WV_B, WK_B), F32)
    ones_a = jnp.ones((b, H_A, ONES_ROWS, t), BF)
    ones_c = jnp.ones((b, ONES_ROWS, t), BF)
    ks, vs, cs, ss, gs = [], [], [], [], []
    y = None
    for l, lw in enumerate(lws):
        (qa, kaf, vaf, kab, vab, qg, kg, vg, gg, la, qm, cp, mla) = _inproj(x.reshape(n, d), lw, cos_t, sin_t, tm, BF)
        r3 = lambda a: a.reshape(b, t, a.shape[-1])
        q_t = jnp.swapaxes(r3(qa), 1, 2)
        v_t = jnp.concatenate([jnp.swapaxes(r3(vab), 1, 2).reshape(b, H_A, DV_A, t), ones_a], axis=2)
        oa = _diff_attn(q_t, r3(kab), v_t.reshape(b, H_A * (DV_A + ONES_ROWS), t), bias_t, lw['gain'], lw['lam'], tq)
        c3 = r3(cp)
        c_t = jnp.concatenate([jnp.swapaxes(c3[:, :, 0:KV_LORA], 1, 2), ones_c], axis=1)
        oc = _mla_attn(jnp.swapaxes(qm.reshape(H_C, b, t, 256), 2, 3), c3, c_t, lw['wuv_t'], tq)
        og, st = _gla(r3(qg), r3(kg), r3(vg), r3(la), r3(gg), s0, lw['gla_norm'], gla_rows, GLA_CHUNK, 16, BF)
        last = l == len(lws) - 1
        res = _ffn(x, oa, og, oc, lw, tm, final_w=p['final_norm'][None, :] if last else None)
        x, gt = res[0], res[1]
        if last:
            y = res[2]
        ks.append(kaf.reshape(b, t, H_A, 2 * DK_A))
        vs.append(vaf.reshape(b, t, H_A, DV_A))
        cs.append(mla.reshape(b, t, KV_LORA + ROPE_C))
        ss.append(_state_from_blockdiag(st))
        gs.append(gt[:, -1, 8 - (CONV_W - 1):, :])
    return y, [jnp.stack(a, axis=0) for a in (ks, vs, cs, ss, gs)]


def _state_to_blockdiag(s):
    b = s.shape[0]
    eye = jnp.eye(H_B, dtype=s.dtype)
    return jnp.einsum('bhdv,hg->bhvgd', s, eye).reshape(b, WV_B, WK_B)


def _state_from_blockdiag(st):
    b = st.shape[0]
    s5 = st.reshape(b, H_B, DV_B, H_B, DK_B)
    diag = jnp.stack([s5[:, hh, :, hh, :] for hh in range(H_B)], axis=1)
    return jnp.swapaxes(diag, -1, -2)


def _sample_trunk(x, p, lws, cache_k, cache_v, cache_c, state_gla, state_conv, page_table):
    b, t, d = x.shape
    n = b * t
    depth, n_pool, page = cache_k.shape[0], cache_k.shape[1], cache_k.shape[2]
    n_pages = page_table.shape[1]
    past_len = n_pages * page
    tm = min(ROW_TILE, n)
    g_pages = min(PAGES_PER_STEP, n_pages)
    assert n % tm == 0 and tm % t == 0 and n_pages % g_pages == 0 and t == 8 and page == MAX_DISTANCE
    pos = past_len + jnp.arange(t, dtype=jnp.int32)
    cos_t, sin_t = _rope_tables(jnp.tile(pos, b))
    kt_cache = jnp.transpose(cache_k, (0, 1, 3, 4, 2)).reshape(depth, n_pool, WA, page)
    vt_cache = jnp.transpose(cache_v, (0, 1, 3, 4, 2)).reshape(depth, n_pool, WV, page)
    ct_cache = jnp.transpose(cache_c, (0, 1, 3, 2))
    page_flat = page_table.reshape(-1).astype(jnp.int32)
    tok = jnp.arange(t, dtype=jnp.int32)
    col = jnp.arange(page, dtype=jnp.int32)
    rel_last = page + tok[:, None] - col[None, :]
    rel_new = jnp.where(col[None, :] < t, tok[:, None] - col[None, :], -1)
    rows = lambda bias: jnp.broadcast_to(bias[:, None], (H_A, 2, t, page)).reshape(2 * H_A * t, page)
    bias_last = jnp.pad(rows(_rel_bias_tile(p['rel_bias'], rel_last)), ((0, 0), ((g_pages - 1) * page, 0)))
    bias_new = rows(_rel_bias_tile(p['rel_bias'], rel_new))
    ks, vs, cs, ss, gs = [], [], [], [], []
    y = None
    x = x.reshape(1, n, d)
    for l, lw in enumerate(lws):
        (qa, kaf, vaf, _, _, qg, kg, vg, gg, la, qm, cp, mla) = _inproj(x.reshape(n, d), lw, cos_t, sin_t, tm, F32)
        r3 = lambda a: a.reshape(b, t, a.shape[-1])
        qm4 = qm.reshape(H_C, b, t, 256)
        rope_q = jnp.stack([qm4[hh, :, :, KV_LORA + hh * ROPE_C:KV_LORA + (hh + 1) * ROPE_C] for hh in range(H_C)])
        qmc = jnp.concatenate([qm4[..., 0:KV_LORA], rope_q], axis=-1)
        oa, oc = _dec_attn(l, page_flat, n_pages, r3(qa), qmc, r3(kaf), r3(vaf), r3(mla),
                           bias_last, bias_new, lw['gain'], lw['lam'], lw['wuv'],
                           kt_cache, vt_cache, ct_cache, g_pages)
        og, st = _gla(r3(qg), r3(kg), r3(vg), r3(la), r3(gg), _state_to_blockdiag(state_gla[l]),
                      lw['gla_norm'], t, t, t, F32)
        prev = jnp.repeat(jnp.swapaxes(state_conv[l], 0, 1), t, axis=1)[:, None]
        last = l == len(lws) - 1
        res = _ffn(x, oa.reshape(1, n, -1), og.reshape(1, n, -1), oc.reshape(1, n, -1), lw, tm, prev=prev,
                   final_w=p['final_norm'][None, :] if last else None)
        x, gt = res[0], res[1]
        if last:
            y = res[2]
        ks.append(kaf.reshape(b, t, H_A, 2 * DK_A))
        vs.append(vaf.reshape(b, t, H_A, DV_A))
        cs.append(mla.reshape(b, t, KV_LORA + ROPE_C))
        ss.append(_state_from_blockdiag(st))
        gs.append(gt.reshape(b, t, -1)[:, t - (CONV_W - 1):, :])
    return y.reshape(b, t, d), [jnp.stack(a, axis=0) for a in (ks, vs, cs, ss, gs)]


def kernel(x_prompt, x_sample, cache_diff_k, cache_diff_v, cache_mla, state_gla, state_ffn_conv, page_table,
           attn_norm, w_in, diff_lambda, diff_subln, rel_bias, w_gla_a2, b_gla_a, gla_norm, mla_q_norm, w_uq,
           mla_kv_norm, w_uk, w_uv, w_out, ffn_norm, w_gate, w_up, conv_w, conv_b, w_down, final_norm):
    p = {'attn_norm': attn_norm, 'w_in': w_in, 'diff_lambda': diff_lambda, 'diff_subln': diff_subln,
         'rel_bias': rel_bias, 'w_gla_a2': w_gla_a2, 'b_gla_a': b_gla_a, 'gla_norm': gla_norm,
         'mla_q_norm': mla_q_norm, 'w_uq': w_uq, 'mla_kv_norm': mla_kv_norm, 'w_uk': w_uk, 'w_uv': w_uv,
         'w_out': w_out, 'ffn_norm': ffn_norm, 'w_gate': w_gate, 'w_up': w_up, 'conv_w': conv_w,
         'conv_b': conv_b, 'w_down': w_down, 'final_norm': final_norm}
    lws = [_prep_layer(l, p) for l in range(w_in.shape[0])]
    y_s, new_s = _sample_trunk(x_sample, p, lws, cache_diff_k, cache_diff_v, cache_mla, state_gla,
                               state_ffn_conv, page_table)
    y_p, new_p = _prompt_trunk(x_prompt, p, lws)
    return (y_p, y_s, *new_p, *new_s)
```
